```python
import math
import jax, jax.numpy as jnp
from jax import lax
import numpy as np

D_MODEL = 2048
BATCH = 4
SEQ = 2048
DEPTH = 1

CHUNK = 64
RMS_EPS = 1e-5
RWKV_DIM = D_MODEL // 2
RWKV_HEAD = 64
RWKV_HEADS = RWKV_DIM // RWKV_HEAD
DECAY_LORA = max(32, int(round(1.8 * RWKV_DIM ** 0.5 / 32)) * 32)
A_LORA = max(32, int(round(1.8 * RWKV_DIM ** 0.5 / 32)) * 32)
G_LORA = max(32, int(round(0.6 * RWKV_DIM ** 0.8 / 32)) * 32)
GN_EPS = 64e-5
MLSTM_DIM = D_MODEL - RWKV_DIM
MLSTM_HEADS = 4
MLSTM_HEAD = MLSTM_DIM // MLSTM_HEADS
CONV_W = 4
HEAD_NORM_EPS = 1e-6
N_EXPERTS = 32
TOP_K = 4
D_FF_EXPERT = D_MODEL
SWIGLU_LIMIT = 7.0
SWIGLU_ALPHA = 1.702
MOE_BLOCK = 256
RWKV_SIZES = (RWKV_DIM, DECAY_LORA, RWKV_DIM, RWKV_DIM, A_LORA, G_LORA)
MLSTM_SIZES = (MLSTM_DIM, MLSTM_DIM, MLSTM_DIM, MLSTM_DIM, MLSTM_HEADS, MLSTM_HEADS)
RWKV_COLS = sum(RWKV_SIZES)
MLSTM_COLS = sum(MLSTM_SIZES)
IN_COLS = RWKV_COLS + MLSTM_COLS

kernel_name = "hybrid_rwkv7_mlstm_moe_adaln_block"


def split_cols(a, sizes):
    offs = np.cumsum(sizes)[:-1].tolist()
    return jnp.split(a, offs, axis=-1)


def rms_norm(x, g):
    x32 = x.astype(jnp.float32)
    y = x32 * lax.rsqrt(jnp.mean(x32 * x32, axis=-1, keepdims=True) + RMS_EPS)
    return y.astype(x.dtype) * g


def token_shift(p):
    return jnp.pad(p[:, :-1], ((0, 0), (1, 0), (0, 0)))


def causal_dwconv(p, w, b):
    T = p.shape[1]
    pp = jnp.pad(p, ((0, 0), (CONV_W - 1, 0), (0, 0)))
    out = b
    for j in range(CONV_W):
        out = out + pp[:, j:j + T] * w[j]
    return out


def rwkv7_recurrence(r, w, k, v, a, b):
    Bsz, T, H, N = r.shape
    xs = tuple(jnp.moveaxis(t, 1, 0) for t in (r, w, k, v, a, b))

    def step(S, inp):
        r_t, w_t, k_t, v_t, a_t, b_t = inp
        Sa = jnp.einsum('bhvk,bhk->bhv', S, a_t)
        S = S * w_t[:, :, None, :] + Sa[..., None] * b_t[:, :, None, :] + v_t[..., None] * k_t[:, :, None, :]
        y = jnp.einsum('bhvk,bhk->bhv', S, r_t)
        return S, y

    S0 = jnp.zeros((Bsz, H, N, N), jnp.float32)
    _, ys = lax.scan(step, S0, xs)
    return jnp.moveaxis(ys, 0, 1)


def rwkv7_mixer(p, mu, w0, w2, a0, a2, g2, k_k, k_a, r_k, ln_w, ln_b):
    Bsz, T, _ = p.shape
    f32 = jnp.float32
    p = p + (token_shift(p) - p) * mu
    r, wl, k, v, al, gl = split_cols(p, RWKV_SIZES)
    w = -jax.nn.softplus(-(w0 + jnp.tanh(wl) @ w2)) - 0.5
    decay = jnp.exp(-jnp.exp(w.astype(f32)))
    a = jax.nn.sigmoid(a0 + al @ a2)
    g = jax.nn.sigmoid(gl) @ g2
    hs = (Bsz, T, RWKV_HEADS, RWKV_HEAD)
    kk = (k * k_k).astype(f32).reshape(hs)
    kk = kk / jnp.maximum(jnp.linalg.norm(kk, axis=-1, keepdims=True), 1e-12)
    k = k * (1.0 + (a - 1.0) * k_a)
    a_h = a.astype(f32).reshape(hs)
    r_h = r.astype(f32).reshape(hs)
    k_h = k.astype(f32).reshape(hs)
    v_h = v.astype(f32).reshape(hs)
    y = rwkv7_recurrence(r_h, decay.reshape(hs), k_h, v_h, -kk, kk * a_h)
    mean = jnp.mean(y, axis=-1, keepdims=True)
    var = jnp.mean(jnp.square(y - mean), axis=-1, keepdims=True)
    y = (y - mean) * lax.rsqrt(var + GN_EPS)
    y = y.reshape(Bsz, T, RWKV_DIM) * ln_w + ln_b
    bonus = jnp.sum(r_h * k_h * r_k.astype(f32), axis=-1, keepdims=True) * v_h
    y = y + bonus.reshape(Bsz, T, RWKV_DIM)
    return y.astype(p.dtype) * g


def mlstm_chunkwise(q, k, v, log_i, log_f):
    Bsz, T, H, DK = q.shape
    DV = v.shape[-1]
    L = CHUNK
    NC = T // L
    f32 = jnp.float32

    def chunks(t):
        return t.astype(f32).reshape(Bsz, NC, L, H, t.shape[-1]).transpose(1, 0, 3, 2, 4)

    def gchunks(t):
        return t.astype(f32).reshape(Bsz, NC, L, H).transpose(1, 0, 3, 2)

    causal = jnp.tril(jnp.ones((L, L), dtype=bool))

    def step(carry, inp):
        C, n, m = carry
        qc, kc, vc, li, lf = inp
        b = jnp.cumsum(lf, axis=-1)
        a_inter = b + m[..., None]
        Dm = b[..., :, None] - b[..., None, :] + li[..., None, :]
        Dm = jnp.where(causal, Dm, -jnp.inf)
        m_t = jnp.maximum(a_inter, jnp.max(Dm, axis=-1))
        w_inter = jnp.exp(a_inter - m_t)
        W = jnp.exp(Dm - m_t[..., None])
        S = jnp.einsum('bhtd,bhsd->bhts', qc, kc) * W
        num = w_inter[..., None] * jnp.einsum('bhvd,bhtd->bhtv', C, qc) + jnp.einsum('bhts,bhsv->bhtv', S, vc)
        den = w_inter * jnp.einsum('bhd,bhtd->bht', n, qc) + jnp.sum(S, axis=-1)
        h = num / jnp.maximum(jnp.abs(den), jnp.exp(-m_t))[..., None]
        m_new = m_t[..., -1]
        g_state = jnp.exp(b[..., -1] + m - m_new)
        w_s = jnp.exp(b[..., -1:] - b + li - m_new[..., None])
        C_new = g_state[..., None, None] * C + jnp.einsum('bhsv,bhsd->bhvd', vc * w_s[..., None], kc)
        n_new = g_state[..., None] * n + jnp.einsum('bhs,bhsd->bhd', w_s, kc)
        return (C_new, n_new, m_new), h

    init = (jnp.zeros((Bsz, H, DV, DK), f32), jnp.zeros((Bsz, H, DK), f32), jnp.zeros((Bsz, H), f32))
    _, hs = lax.scan(step, init, (chunks(q), chunks(k), chunks(v), gchunks(log_i), gchunks(log_f)))
    return hs.transpose(1, 0, 3, 2, 4).reshape(Bsz, T, H, DV)


def mlstm_mixer(p, conv_w, conv_b, b_i, b_f, norm_g):
    Bsz, T, _ = p.shape
    f32 = jnp.float32
    qk, v, o, gi, gf = split_cols(p, (2 * MLSTM_DIM, MLSTM_DIM, MLSTM_DIM, MLSTM_HEADS, MLSTM_HEADS))
    qk = jax.nn.silu(causal_dwconv(qk, conv_w, conv_b))
    q, k = jnp.split(qk, 2, axis=-1)
    q = q.reshape(Bsz, T, MLSTM_HEADS, MLSTM_HEAD)
    k = k.reshape(Bsz, T, MLSTM_HEADS, MLSTM_HEAD) * (MLSTM_HEAD ** -0.5)
    v = v.reshape(Bsz, T, MLSTM_HEADS, MLSTM_HEAD)
    log_i = (gi + b_i).astype(f32)
    log_f = jax.nn.log_sigmoid((gf + b_f).astype(f32))
    h = mlstm_chunkwise(q, k, v, log_i, log_f)
    h = h * lax.rsqrt(jnp.mean(h * h, axis=-1, keepdims=True) + HEAD_NORM_EPS)
    h = h.reshape(Bsz, T, MLSTM_DIM).astype(p.dtype) * norm_g
    return h * jax.nn.sigmoid(o)


def moe_ffn(h, router_w, router_b, w_gu, b_gu, w_dn, b_dn):
    N, D = h.shape
    F = D_FF_EXPERT
    logits = (h @ router_w + router_b).astype(jnp.float32)
    top_val, top_idx = lax.top_k(logits, TOP_K)
    gates = jax.nn.softmax(top_val, axis=-1).astype(h.dtype)
    NA = N * TOP_K
    flat_e = top_idx.reshape(-1).astype(jnp.int32)
    flat_tok = jnp.arange(NA, dtype=jnp.int32) // TOP_K
    flat_g = gates.reshape(-1)
    order = jnp.argsort(flat_e)
    se, stok, sg = flat_e[order], flat_tok[order], flat_g[order]
    counts = jnp.bincount(flat_e, length=N_EXPERTS).astype(jnp.int32)
    starts = jnp.cumsum(counts) - counts
    padded = (counts + MOE_BLOCK - 1) // MOE_BLOCK * MOE_BLOCK
    pends = jnp.cumsum(padded)
    pstarts = pends - padded
    dest = pstarts[se] + (jnp.arange(NA, dtype=jnp.int32) - starts[se])
    n_blocks = (NA + MOE_BLOCK - 1) // MOE_BLOCK + N_EXPERTS
    R = n_blocks * MOE_BLOCK
    row_tok = jnp.zeros((R,), jnp.int32).at[dest].set(stok)
    row_g = jnp.zeros((R,), h.dtype).at[dest].set(sg)
    block_start = jnp.arange(n_blocks, dtype=jnp.int32) * MOE_BLOCK
    block_e = jnp.minimum(jnp.searchsorted(pends, block_start, side='right'), N_EXPERTS - 1).astype(jnp.int32)

    def expert_block(args):
        tok, e = args
        xb = h[tok]
        gu = xb @ w_gu[e] + b_gu[e]
        gate, up = gu[:, :F], gu[:, F:]
        gate = jnp.minimum(gate, SWIGLU_LIMIT)
        up = jnp.clip(up, -SWIGLU_LIMIT, SWIGLU_LIMIT)
        act = (up + 1.0) * (gate * jax.nn.sigmoid(SWIGLU_ALPHA * gate))
        return act @ w_dn[e] + b_dn[e]

    out = lax.map(expert_block, (row_tok.reshape(n_blocks, MOE_BLOCK), block_e))
    out = out.reshape(R, D) * row_g[:, None]
    return jnp.zeros((N, D), out.dtype).at[row_tok].add(out)


def setup_inputs(seed: int = 0) -> dict:
    key = jax.random.key(seed)
    ks = iter(jax.random.split(key, 48))
    f32 = jnp.float32
    L, D, E, F = DEPTH, D_MODEL, N_EXPERTS, D_FF_EXPERT

    def nrm(shape, s):
        return jax.random.normal(next(ks), shape, f32) * s

    def unif(shape):
        return jax.random.uniform(next(ks), shape, f32)

    return {
        'x': nrm((BATCH, SEQ, D), 1.0),
        'c': nrm((BATCH, D), 1.0),
        'ada_w': nrm((L, D, 6 * D), 0.5 * D ** -0.5),
        'ada_b': nrm((L, 6 * D), 0.02),
        'norm1_g': 1.0 + nrm((L, D), 0.02),
        'w_in': nrm((L, D, IN_COLS), D ** -0.5),
        'rwkv_mu': unif((L, RWKV_COLS)),
        'rwkv_w0': -2.0 + nrm((L, RWKV_DIM), 0.5),
        'rwkv_w2': nrm((L, DECAY_LORA, RWKV_DIM), 0.1),
        'rwkv_a0': nrm((L, RWKV_DIM), 0.1),
        'rwkv_a2': nrm((L, A_LORA, RWKV_DIM), 0.1),
        'rwkv_g2': nrm((L, G_LORA, RWKV_DIM), G_LORA ** -0.5),
        'rwkv_kk': 0.85 + nrm((L, RWKV_DIM), 0.05),
        'rwkv_ka': 1.0 + nrm((L, RWKV_DIM), 0.05),
        'rwkv_rk': nrm((L, RWKV_HEADS, RWKV_HEAD), 0.1),
        'rwkv_ln_w': 1.0 + nrm((L, RWKV_DIM), 0.02),
        'rwkv_ln_b': nrm((L, RWKV_DIM), 0.02),
        'mlstm_conv_w': nrm((L, CONV_W, 2 * MLSTM_DIM), CONV_W ** -0.5),
        'mlstm_conv_b': nrm((L, 2 * MLSTM_DIM), 0.02),
        'mlstm_b_i': -1.0 + nrm((L, MLSTM_HEADS), 0.1),
        'mlstm_b_f': 3.0 + nrm((L, MLSTM_HEADS), 0.5),
        'mlstm_norm_g': 1.0 + nrm((L, MLSTM_DIM), 0.02),
        'w_out': nrm((L, D, D), D ** -0.5),
        'norm2_g': 1.0 + nrm((L, D), 0.02),
        'router_w': nrm((L, D, E), D ** -0.5),
        'router_b': nrm((L, E), 0.01),
        'moe_w_gu': nrm((L, E, D, 2 * F), D ** -0.5),
        'moe_b_gu': nrm((L, E, 2 * F), 0.01),
        'moe_w_dn': nrm((L, E, F, D), F ** -0.5),
        'moe_b_dn': nrm((L, E, D), 0.01),
        'final_g': 1.0 + nrm((D,), 0.02),
    }


def reference(x, c, ada_w, ada_b, norm1_g, w_in, rwkv_mu, rwkv_w0, rwkv_w2, rwkv_a0, rwkv_a2,
              rwkv_g2, rwkv_kk, rwkv_ka, rwkv_rk, rwkv_ln_w, rwkv_ln_b, mlstm_conv_w, mlstm_conv_b,
              mlstm_b_i, mlstm_b_f, mlstm_norm_g, w_out, norm2_g, router_w, router_b,
              moe_w_gu, moe_b_gu, moe_w_dn, moe_b_dn, final_g):
    Bsz, T, D = x.shape
    for l in range(DEPTH):
        mod = (jax.nn.silu(c) @ ada_w[l] + ada_b[l])[:, None, :]
        sh_m, sc_m, gt_m, sh_f, sc_f, gt_f = jnp.split(mod, 6, axis=-1)

        h = rms_norm(x, norm1_g[l]) * (1.0 + sc_m) + sh_m
        proj = h @ w_in[l]
        p_rwkv, p_mlstm = proj[..., :RWKV_COLS], proj[..., RWKV_COLS:]
        y_rwkv = rwkv7_mixer(p_rwkv, rwkv_mu[l], rwkv_w0[l], rwkv_w2[l], rwkv_a0[l], rwkv_a2[l],
                             rwkv_g2[l], rwkv_kk[l], rwkv_ka[l], rwkv_rk[l], rwkv_ln_w[l], rwkv_ln_b[l])
        y_mlstm = mlstm_mixer(p_mlstm, mlstm_conv_w[l], mlstm_conv_b[l], mlstm_b_i[l], mlstm_b_f[l],
                              mlstm_norm_g[l])
        mix = jnp.concatenate([y_rwkv, y_mlstm], axis=-1) @ w_out[l]
        x = x + gt_m * mix

        h = rms_norm(x, norm2_g[l]) * (1.0 + sc_f) + sh_f
        y = moe_ffn(h.reshape(Bsz * T, D), router_w[l], router_b[l], moe_w_gu[l], moe_b_gu[l],
                    moe_w_dn[l], moe_b_dn[l])
        x = x + gt_f * y.reshape(Bsz, T, D)
    return rms_norm(x, final_g)
```

```python
import functools

import jax
import jax.numpy as jnp
from jax import lax
from jax.experimental import pallas as pl
from jax.experimental.pallas import tpu as pltpu

F32 = jnp.float32
BF16 = jnp.bfloat16
HIGHEST = lax.Precision.HIGHEST

D_MODEL = 2048
CHUNK = 64
RMS_EPS = 1e-5
RWKV_DIM = 1024
RWKV_HEAD = 64
DECAY_LORA = 64
A_LORA = 64
G_LORA = 160
GN_EPS = 64e-5
MLSTM_DIM = 1024
MLSTM_HEADS = 4
MLSTM_HEAD = 256
CONV_W = 4
HEAD_NORM_EPS = 1e-6
N_EXPERTS = 32
TOP_K = 4
D_FF = 2048
SWIGLU_LIMIT = 7.0
SWIGLU_ALPHA = 1.702
MOE_BLOCK = 256

COL_R, COL_K, COL_V, COL_Q, COL_MK, COL_MV, COL_MO = (i * 1024 for i in range(7))
COL_GL = 7168
COL_LORA = 7424
COL_IF = 7552
IN_COLS_P = 7680
QUAD = 256

VMEM_LIMIT = 56 * 1024 * 1024


def _cparams(sem, vmem=VMEM_LIMIT):
    return pltpu.CompilerParams(dimension_semantics=sem, vmem_limit_bytes=vmem)


def _dot(a, b, **kw):
    return jnp.dot(a, b, preferred_element_type=F32, **kw)


def _dot_nt(a, b, **kw):
    return lax.dot_general(a, b, (((1,), (1,)), ((), ())), preferred_element_type=F32, **kw)


def _dot_tn(a, b, **kw):
    return lax.dot_general(a, b, (((0,), (0,)), ((), ())), preferred_element_type=F32, **kw)


def _split3(x):
    h = x.astype(BF16)
    r1 = x - h.astype(F32)
    m = r1.astype(BF16)
    l = (r1 - m.astype(F32)).astype(BF16)
    return h, m, l


def _iota(shape, dim):
    return lax.broadcasted_iota(jnp.int32, shape, dim)


def _adaln_kernel(c_ref, w_ref, b_ref, o_ref):
    c = c_ref[...]
    s = c * jax.nn.sigmoid(c)
    o_ref[...] = _dot(s, w_ref[...], precision=HIGHEST) + b_ref[...]


def _adaln(c, ada_w, ada_b):
    B, D = c.shape
    n_out = ada_w.shape[1]
    tn = 1024
    c8 = jnp.zeros((8, D), F32).at[:B].set(c)
    out = pl.pallas_call(
        _adaln_kernel,
        grid=(n_out // tn,),
        in_specs=[pl.BlockSpec((8, D), lambda j: (0, 0)),
                  pl.BlockSpec((D, tn), lambda j: (0, j)),
                  pl.BlockSpec((1, tn), lambda j: (0, j))],
        out_specs=pl.BlockSpec((8, tn), lambda j: (0, j)),
        out_shape=jax.ShapeDtypeStruct((8, n_out), F32),
        compiler_params=_cparams(("parallel",)),
        name="adaln",
    )(c8, ada_w, ada_b.reshape(1, n_out))
    return out[:B]


def _inproj_kernel(x_ref, g_ref, sc_ref, sh_ref, w_ref, o_ref, h_scr):
    @pl.when(pl.program_id(1) == 0)
    def _():
        x = x_ref[...]
        y = x * lax.rsqrt(jnp.mean(x * x, axis=-1, keepdims=True) + RMS_EPS)
        h = (y * g_ref[...]) * (1.0 + sc_ref[0]) + sh_ref[0]
        h_scr[...] = h.astype(BF16)

    o_ref[...] = _dot(h_scr[...], w_ref[...])


def _in_proj(x2, norm_g, sc, sh, w_p, seq):
    N, D = x2.shape
    tm, tn = 1024, 768
    tiles_per_b = seq // tm
    ncols = w_p.shape[1]
    return pl.pallas_call(
        _inproj_kernel,
        grid=(N // tm, ncols // tn),
        in_specs=[pl.BlockSpec((tm, D), lambda i, j: (i, 0)),
                  pl.BlockSpec((1, D), lambda i, j: (0, 0)),
                  pl.BlockSpec((1, 1, D), lambda i, j: (i // tiles_per_b, 0, 0)),
                  pl.BlockSpec((1, 1, D), lambda i, j: (i // tiles_per_b, 0, 0)),
                  pl.BlockSpec((D, tn), lambda i, j: (0, j))],
        out_specs=pl.BlockSpec((tm, tn), lambda i, j: (i, j)),
        out_shape=jax.ShapeDtypeStruct((N, ncols), F32),
        scratch_shapes=[pltpu.VMEM((tm, D), BF16)],
        compiler_params=_cparams(("parallel", "arbitrary")),
        name="in_proj",
    )(x2, norm_g.reshape(1, D), sc, sh, w_p)


def _shift_rows(x, prev_row):
    rolled = pltpu.roll(x, 1, axis=0)
    row = _iota(x.shape, 0)
    return jnp.where(row == 0, prev_row, rolled)


def _rwkv_kernel(r_ref, k_ref, v_ref, gl_ref, lo_ref,
                 mu_r_ref, mu_k_ref, mu_v_ref, mu_gl_ref, mu_lo_ref,
                 w0_ref, a0_ref, kkp_ref, kap_ref, rk_ref, lnw_ref, lnb_ref,
                 w2_ref, a2_ref, g2_ref,
                 o_ref,
                 s_scr, prev_scr):
    L = CHUNK
    c = pl.program_id(1)

    @pl.when(c == 0)
    def _():
        s_scr[...] = jnp.zeros_like(s_scr)
        prev_scr[...] = jnp.zeros_like(prev_scr)

    def lerp(ref, mu_ref, lo, width):
        x = ref[...]
        prev = prev_scr[0:1, lo:lo + width]
        out = x + (_shift_rows(x, prev) - x) * mu_ref[...]
        prev_scr[0:1, lo:lo + width] = x[L - 1:L, :]
        return out

    r = lerp(r_ref, mu_r_ref, 0, RWKV_DIM)
    k = lerp(k_ref, mu_k_ref, 1024, RWKV_DIM)
    v = lerp(v_ref, mu_v_ref, 2048, RWKV_DIM)
    gl = lerp(gl_ref, mu_gl_ref, 3072, 256)
    lo = lerp(lo_ref, mu_lo_ref, 3328, 128)

    w = -jax.nn.softplus(-(w0_ref[...] + _dot(jnp.tanh(lo), w2_ref[...], precision=HIGHEST))) - 0.5
    lw = -jnp.exp(w)
    a = jax.nn.sigmoid(a0_ref[...] + _dot(lo, a2_ref[...], precision=HIGHEST))
    g = _dot(jax.nn.sigmoid(gl).astype(BF16), g2_ref[...])
    kkraw = k * kkp_ref[...]
    k2 = k * (1.0 + (a - 1.0) * kap_ref[...])

    row = _iota((L, L), 0)
    col = _iota((L, L), 1)
    strict = col < row
    incl = col <= row
    ltri = jnp.where(incl, 1.0, 0.0).astype(BF16)
    qr = _iota((QUAD, QUAD), 0) // RWKV_HEAD
    qc = _iota((QUAD, QUAD), 1) // RWKV_HEAD
    bd_mask = qr == qc
    bd = jnp.where(bd_mask, 1.0, 0.0).astype(BF16)
    lane_head = _iota((1, QUAD), 1) // RWKV_HEAD

    def head_sum(x):
        h, m, l = _split3(x)
        return _dot(h, bd) + _dot(m, bd) + _dot(l, bd)

    def time_cumsum(x):
        h, m, l = _split3(x)
        return _dot(ltri, h) + _dot(ltri, m) + _dot(ltri, l)

    for q in range(RWKV_DIM // QUAD):
        sl = slice(q * QUAD, (q + 1) * QUAD)
        r_q, lw_q, k_q, v_q, a_q, g_q = r[:, sl], lw[:, sl], k2[:, sl], v[:, sl], a[:, sl], g[:, sl]
        kkr = kkraw[:, sl]
        kk = kkr / jnp.maximum(jnp.sqrt(head_sum(kkr * kkr)), 1e-12)

        cl = time_cumsum(lw_q)
        p_t = jnp.exp(cl)
        p_prev = jnp.exp(cl - lw_q)
        p_inv = jnp.exp(-cl)
        at = -kk * p_prev
        bt = (kk * a_q * p_inv).astype(BF16)
        kt = (k_q * p_inv).astype(BF16)
        rt = r_q * p_t

        s0 = s_scr[q]
        lhs = jnp.concatenate([at, rt], axis=0).astype(BF16)
        xy = _dot_nt(lhs, s0.astype(BF16))
        x_acc = xy[:L]
        y_acc = xy[L:]

        n_pow, a_rb, v_m = [], [], []
        for h in range(QUAD // RWKV_HEAD):
            mh = lane_head == h
            lh = jnp.where(mh, lhs, jnp.zeros_like(lhs))
            g1 = _dot_nt(lh, bt)
            g2 = _dot_nt(lh, kt)
            n_pow.append(jnp.where(strict, g1[:L], 0.0))
            a_rb.append(jnp.where(incl, g1[L:], 0.0).astype(BF16))
            a_ak = jnp.where(strict, g2[:L], 0.0).astype(BF16)
            a_rk = jnp.where(incl, g2[L:], 0.0).astype(BF16)
            vm = jnp.where(mh, v_q, 0.0).astype(BF16)
            x_acc = x_acc + _dot(a_ak, vm)
            y_acc = y_acc + _dot(a_rk, vm)

        u = x_acc
        for j in range(6):
            upd = None
            for h in range(QUAD // RWKV_HEAD):
                mh = lane_head == h
                nb = n_pow[h].astype(BF16)
                t = _dot(nb, jnp.where(mh, u, 0.0).astype(BF16))
                upd = t if upd is None else upd + t
                if j < 5:
                    n_pow[h] = _dot(nb, nb)
            u = u + upd

        for h in range(QUAD // RWKV_HEAD):
            mh = lane_head == h
            y_acc = y_acc + _dot(a_rb[h], jnp.where(mh, u, 0.0).astype(BF16))

        z = s0 + _dot_tn(u.astype(BF16), bt) + _dot_tn(v_q.astype(BF16), kt)
        s_scr[q] = jnp.where(bd_mask, z, 0.0) * p_t[L - 1:L, :]

        inv_n = 1.0 / RWKV_HEAD
        mean = head_sum(y_acc) * inv_n
        d = y_acc - mean
        var = head_sum(d * d) * inv_n
        yn = d * lax.rsqrt(var + GN_EPS) * lnw_ref[:, sl] + lnb_ref[:, sl]
        bonus = head_sum(r_q * k_q * rk_ref[:, sl]) * v_q
        o_ref[:, sl] = ((yn + bonus) * g_q).astype(o_ref.dtype)


def _rwkv(proj, B, T, pr):
    NC = T // CHUNK
    L = CHUNK
    row = lambda b, c: b * NC + c

    def pspec(width, colblk):
        return pl.BlockSpec((L, width), lambda b, c: (row(b, c), colblk))

    def vec(width):
        return pl.BlockSpec((1, width), lambda b, c: (0, 0))

    def mat(rows):
        return pl.BlockSpec((rows, RWKV_DIM), lambda b, c: (0, 0))

    return pl.pallas_call(
        _rwkv_kernel,
        grid=(B, NC),
        in_specs=[pspec(1024, COL_R // 1024), pspec(1024, COL_K // 1024), pspec(1024, COL_V // 1024),
                  pspec(256, COL_GL // 256), pspec(128, COL_LORA // 128),
                  vec(1024), vec(1024), vec(1024), vec(256), vec(128),
                  vec(1024), vec(1024), vec(1024), vec(1024), vec(1024), vec(1024), vec(1024),
                  mat(128), mat(128), mat(256)],
        out_specs=pl.BlockSpec((L, RWKV_DIM), lambda b, c: (row(b, c), 0)),
        out_shape=jax.ShapeDtypeStruct((B * T, RWKV_DIM), BF16),
        scratch_shapes=[pltpu.VMEM((RWKV_DIM // QUAD, QUAD, QUAD), F32),
                        pltpu.VMEM((8, 3072 + 256 + 128), F32)],
        compiler_params=_cparams(("parallel", "arbitrary")),
        name="rwkv",
    )(proj, proj, proj, proj, proj,
      pr["mu_r"], pr["mu_k"], pr["mu_v"], pr["mu_gl"], pr["mu_lo"],
      pr["w0"], pr["a0"], pr["kk"], pr["ka"], pr["rk"], pr["ln_w"], pr["ln_b"],
      pr["w2"], pr["a2"], pr["g2"])


def _mlstm_kernel(q_ref, k_ref, v_ref, o_ref, gt_ref,
                  cwq_ref, cwk_ref, cbq_ref, cbk_ref, gbias_ref, ng_ref,
                  out_ref,
                  c_scr, n_scr, m_scr, prevq_scr, prevk_scr):
    L = CHUNK
    c = pl.program_id(1)

    @pl.when(c == 0)
    def _():
        c_scr[...] = jnp.zeros_like(c_scr)
        n_scr[...] = jnp.zeros_like(n_scr)
        m_scr[...] = jnp.zeros_like(m_scr)
        prevq_scr[...] = jnp.zeros_like(prevq_scr)
        prevk_scr[...] = jnp.zeros_like(prevk_scr)

    def conv_silu(ref, prev_scr, cw_ref, cb_ref):
        x = ref[...]
        prev8 = prev_scr[...]
        row8 = _iota(prev8.shape, 0)
        out = cb_ref[...] + x * cw_ref[CONV_W - 1:CONV_W, :]
        for d in range(1, CONV_W):
            rolled = pltpu.roll(x, d, axis=0)
            first8 = jnp.where(row8 < d, pltpu.roll(prev8, d, axis=0), rolled[:8])
            xs = jnp.concatenate([first8, rolled[8:]], axis=0)
            out = out + xs * cw_ref[CONV_W - 1 - d:CONV_W - d, :]
        prev_scr[...] = x[L - 8:L, :]
        return out * jax.nn.sigmoid(out)

    qf = conv_silu(q_ref, prevq_scr, cwq_ref, cbq_ref)
    kf = conv_silu(k_ref, prevk_scr, cwk_ref, cbk_ref) * (MLSTM_HEAD ** -0.5)
    vf = v_ref[...]

    z = gt_ref[...] + gbias_ref[...]
    lane = _iota(z.shape, 1)
    gc = jnp.where(lane < MLSTM_HEADS, z,
                   jnp.where(lane < 2 * MLSTM_HEADS, jax.nn.log_sigmoid(z), 0.0))
    row = _iota((L, L), 0)
    col = _iota((L, L), 1)
    causal = col <= row
    ltri = jnp.where(causal, 1.0, 0.0).astype(BF16)
    gh, gm, glo = _split3(gc)
    cum = _dot(ltri, gh) + _dot(ltri, gm) + _dot(ltri, glo)
    gc_t = gc.T
    cum_t = cum.T

    for h in range(MLSTM_HEADS):
        sl = slice(h * MLSTM_HEAD, (h + 1) * MLSTM_HEAD)
        qh, kh, vh = qf[:, sl], kf[:, sl], vf[:, sl]
        li_row = gc_t[h:h + 1, :]
        li_col = gc[:, h:h + 1]
        b_row = cum_t[MLSTM_HEADS + h:MLSTM_HEADS + h + 1, :]
        b_col = cum[:, MLSTM_HEADS + h:MLSTM_HEADS + h + 1]
        m_prev = m_scr[h:h + 1, 0:1]
        c_prev = c_scr[h]
        n_prev = n_scr[h:h + 1, :]

        a_inter = b_col + m_prev
        dm = jnp.where(causal, b_col - b_row + li_row, -jnp.inf)
        m_t = jnp.maximum(a_inter, jnp.max(dm, axis=-1, keepdims=True))
        w_inter = jnp.exp(a_inter - m_t)
        wmat = jnp.exp(dm - m_t)
        qb, kb, vb = qh.astype(BF16), kh.astype(BF16), vh.astype(BF16)
        s = _dot_nt(qb, kb) * wmat
        num = w_inter * _dot_nt(qb, c_prev.astype(BF16)) + _dot(s.astype(BF16), vb)
        den = w_inter * jnp.sum(qh * n_prev, axis=-1, keepdims=True) + jnp.sum(s, axis=-1, keepdims=True)
        hh = num / jnp.maximum(jnp.abs(den), jnp.exp(-m_t))

        m_new = m_t[L - 1:L, :]
        b_last = b_col[L - 1:L, :]
        g_state = jnp.exp(b_last + m_prev - m_new)
        w_s = jnp.exp(b_last - b_col + li_col - m_new)
        c_scr[h] = g_state * c_prev + _dot_tn((vh * w_s).astype(BF16), kb)
        n_scr[h:h + 1, :] = g_state * n_prev + jnp.sum(kh * w_s, axis=0, keepdims=True)
        m_scr[h:h + 1, :] = jnp.broadcast_to(m_new, (1, m_scr.shape[1]))

        hn = hh * lax.rsqrt(jnp.mean(hh * hh, axis=-1, keepdims=True) + HEAD_NORM_EPS)
        out_ref[:, sl] = ((hn * ng_ref[:, sl]) * jax.nn.sigmoid(o_ref[:, sl])).astype(out_ref.dtype)


def _mlstm(proj, B, T, pr):
    NC = T // CHUNK
    L = CHUNK
    row = lambda b, c: b * NC + c

    def pspec(width, colblk):
        return pl.BlockSpec((L, width), lambda b, c: (row(b, c), colblk))

    def cst(shape):
        return pl.BlockSpec(shape, lambda b, c: (0, 0))

    return pl.pallas_call(
        _mlstm_kernel,
        grid=(B, NC),
        in_specs=[pspec(1024, COL_Q // 1024), pspec(1024, COL_MK // 1024), pspec(1024, COL_MV // 1024),
                  pspec(1024, COL_MO // 1024), pspec(128, COL_IF // 128),
                  cst((CONV_W, 1024)), cst((CONV_W, 1024)), cst((1, 1024)), cst((1, 1024)),
                  cst((1, 128)), cst((1, 1024))],
        out_specs=pl.BlockSpec((L, MLSTM_DIM), lambda b, c: (row(b, c), 0)),
        out_shape=jax.ShapeDtypeStruct((B * T, MLSTM_DIM), BF16),
        scratch_shapes=[pltpu.VMEM((MLSTM_HEADS, MLSTM_HEAD, MLSTM_HEAD), F32),
                        pltpu.VMEM((8, MLSTM_HEAD), F32),
                        pltpu.VMEM((8, 128), F32),
                        pltpu.VMEM((8, 1024), F32),
                        pltpu.VMEM((8, 1024), F32)],
        compiler_params=_cparams(("parallel", "arbitrary")),
        name="mlstm",
    )(proj, proj, proj, proj, proj,
      pr["cw_q"], pr["cw_k"], pr["cb_q"], pr["cb_k"], pr["gbias"], pr["norm_g"])


def _out_route_kernel(yr_ref, ym_ref, x_ref, wo_ref, gt_ref, g2_ref, sc_ref, sh_ref, rw_ref, rb_ref,
                      x1_ref, h2_ref, idx_ref, gate_ref, rank_ref, cnt_ref,
                      carry_scr):
    i = pl.program_id(0)
    tm = x_ref.shape[0]

    @pl.when(i == 0)
    def _():
        carry_scr[...] = jnp.zeros_like(carry_scr)

    mix = _dot(yr_ref[...], wo_ref[0:RWKV_DIM, :]) + _dot(ym_ref[...], wo_ref[RWKV_DIM:, :])
    x1 = x_ref[...] + gt_ref[0] * mix
    x1_ref[...] = x1
    y = x1 * lax.rsqrt(jnp.mean(x1 * x1, axis=-1, keepdims=True) + RMS_EPS)
    h2 = (y * g2_ref[...]) * (1.0 + sc_ref[0]) + sh_ref[0]
    h2_ref[...] = h2

    logits = _dot(h2, rw_ref[...], precision=HIGHEST) + rb_ref[...]
    lt = logits.T[:N_EXPERTS, :]
    e_iota = _iota(lt.shape, 0)
    onehots, vals, idxs = [], [], []
    for _ in range(TOP_K):
        mx = jnp.max(lt, axis=0, keepdims=True)
        idx = jnp.min(jnp.where(lt == mx, e_iota, N_EXPERTS), axis=0, keepdims=True)
        sel = e_iota == idx
        onehots.append(sel)
        vals.append(mx)
        idxs.append(idx)
        lt = jnp.where(sel, -jnp.inf, lt)
    exps = [jnp.exp(vv - vals[0]) for vv in vals]
    denom = exps[0] + exps[1] + exps[2] + exps[3]
    gates = [e / denom for e in exps]

    member = jnp.zeros(onehots[0].shape, F32)
    for sel in onehots:
        member = member + jnp.where(sel, 1.0, 0.0)
    ur = _iota((tm, tm), 0)
    uc = _iota((tm, tm), 1)
    ustrict = jnp.where(ur < uc, 1.0, 0.0).astype(BF16)
    before = _dot(member.astype(BF16), ustrict) + carry_scr[:, 0:1]
    for j in range(TOP_K):
        rank = jnp.sum(jnp.where(onehots[j], before, 0.0), axis=0, keepdims=True)
        rank_ref[j:j + 1, :] = rank.astype(jnp.int32)
        idx_ref[j:j + 1, :] = idxs[j]
    new_carry = carry_scr[...] + jnp.sum(member, axis=1, keepdims=True)
    carry_scr[...] = new_carry
    cnt_ref[...] = new_carry.astype(jnp.int32)

    grows = jnp.concatenate(gates + [jnp.zeros((128 - TOP_K, tm), F32)], axis=0)
    gate_ref[...] = grows.T


def _out_route(yr, ym, x2, w_out_b, gt, g2, sc, sh, rw_p, rb_p, seq):
    N, D = x2.shape
    tm = 512
    tiles_per_b = seq // tm
    bvec = pl.BlockSpec((1, 1, D), lambda i: (i // tiles_per_b, 0, 0))
    return pl.pallas_call(
        _out_route_kernel,
        grid=(N // tm,),
        in_specs=[pl.BlockSpec((tm, RWKV_DIM), lambda i: (i, 0)),
                  pl.BlockSpec((tm, MLSTM_DIM), lambda i: (i, 0)),
                  pl.BlockSpec((tm, D), lambda i: (i, 0)),
                  pl.BlockSpec((D, D), lambda i: (0, 0)),
                  bvec,
                  pl.BlockSpec((1, D), lambda i: (0, 0)),
                  bvec, bvec,
                  pl.BlockSpec((D, 128), lambda i: (0, 0)),
                  pl.BlockSpec((1, 128), lambda i: (0, 0))],
        out_specs=[pl.BlockSpec((tm, D), lambda i: (i, 0)),
                   pl.BlockSpec((tm, D), lambda i: (i, 0)),
                   pl.BlockSpec((TOP_K, tm), lambda i: (0, i)),
                   pl.BlockSpec((tm, 128), lambda i: (i, 0)),
                   pl.BlockSpec((TOP_K, tm), lambda i: (0, i)),
                   pl.BlockSpec((N_EXPERTS, 128), lambda i: (0, 0))],
        out_shape=[jax.ShapeDtypeStruct((N, D), F32),
                   jax.ShapeDtypeStruct((N, D), F32),
                   jax.ShapeDtypeStruct((TOP_K, N), jnp.int32),
                   jax.ShapeDtypeStruct((N, 128), F32),
                   jax.ShapeDtypeStruct((TOP_K, N), jnp.int32),
                   jax.ShapeDtypeStruct((N_EXPERTS, 128), jnp.int32)],
        scratch_shapes=[pltpu.VMEM((N_EXPERTS, 128), F32)],
        compiler_params=_cparams(("arbitrary",)),
        name="out_route",
    )(yr, ym, x2, w_out_b, gt, g2.reshape(1, D), sc, sh, rw_p, rb_p)


def _dest_kernel(idx_ref, rank_ref, cnt_ref, dest_ref):
    cnt = cnt_ref[:, 0:1].astype(F32)
    padded = jnp.floor((cnt + (MOE_BLOCK - 1)) * (1.0 / MOE_BLOCK)) * MOE_BLOCK
    er = _iota((N_EXPERTS, N_EXPERTS), 0)
    ec = _iota((N_EXPERTS, N_EXPERTS), 1)
    before = jnp.where(ec < er, 1.0, 0.0)
    pstart = _dot(before, jnp.broadcast_to(padded, (N_EXPERTS, 128)), precision=HIGHEST)[:, 0:1]
    e_iota = _iota((N_EXPERTS, idx_ref.shape[1]), 0)
    for j in range(TOP_K):
        sel = e_iota == idx_ref[j:j + 1, :]
        base = jnp.sum(jnp.where(sel, pstart, 0.0), axis=0, keepdims=True)
        dest_ref[j:j + 1, :] = base.astype(jnp.int32) + rank_ref[j:j + 1, :]


def _dest(idx, rank, cnt):
    N = idx.shape[1]
    tn = 2048
    return pl.pallas_call(
        _dest_kernel,
        grid=(N // tn,),
        in_specs=[pl.BlockSpec((TOP_K, tn), lambda i: (0, i)),
                  pl.BlockSpec((TOP_K, tn), lambda i: (0, i)),
                  pl.BlockSpec((N_EXPERTS, 128), lambda i: (0, 0))],
        out_specs=pl.BlockSpec((TOP_K, tn), lambda i: (0, i)),
        out_shape=jax.ShapeDtypeStruct((TOP_K, N), jnp.int32),
        compiler_params=_cparams(("parallel",)),
        name="dest",
    )(idx, rank, cnt)


def _dispatch_kernel(dest_ref, h_ref, xs_in_ref, xs_ref, sem):
    del xs_in_ref
    i = pl.program_id(0)
    tm = h_ref.shape[0]
    n_tok = pl.num_programs(0) * tm

    def copy(t, j):
        d = dest_ref[j * n_tok + i * tm + t]
        return pltpu.make_async_copy(h_ref.at[pl.ds(t, 1)], xs_ref.at[pl.ds(d, 1)], sem)

    def issue(t, carry):
        for j in range(TOP_K):
            copy(t, j).start()
        return carry

    lax.fori_loop(0, tm, issue, 0)

    def drain(t, carry):
        for j in range(TOP_K):
            copy(t, j).wait()
        return carry

    lax.fori_loop(0, tm, drain, 0)


def _dispatch(dest_flat, h2, n_rows):
    N, D = h2.shape
    tm = 256
    xs0 = jnp.zeros((n_rows, D), h2.dtype)
    return pl.pallas_call(
        _dispatch_kernel,
        grid_spec=pltpu.PrefetchScalarGridSpec(
            num_scalar_prefetch=1,
            grid=(N // tm,),
            in_specs=[pl.BlockSpec((tm, D), lambda i, d: (i, 0)),
                      pl.BlockSpec(memory_space=pl.ANY)],
            out_specs=pl.BlockSpec(memory_space=pl.ANY),
            scratch_shapes=[pltpu.SemaphoreType.DMA(())]),
        out_shape=jax.ShapeDtypeStruct((n_rows, D), h2.dtype),
        input_output_aliases={2: 0},
        compiler_params=_cparams(("arbitrary",)),
        name="dispatch",
    )(dest_flat, h2, xs0)


GROUP_ROWS = 1024
FF_TILE = 256


def _experts_kernel(ge_ref, gs_ref, gn_ref,
                    xs_ref, wg_ref, wu_ref, wd_ref, bg_ref, bu_ref, bd_ref, ys_in_ref,
                    ys_ref,
                    xbuf, acc, wgb, wub, wdb, sem):
    del ys_in_ref
    g = pl.program_id(0)
    j = pl.program_id(1)
    nblk = gn_ref[g]
    row0 = pl.multiple_of(gs_ref[g] * MOE_BLOCK, MOE_BLOCK)

    @pl.when(nblk > 0)
    def _():
        @pl.when(j == 0)
        def _():
            cp = pltpu.make_async_copy(xs_ref.at[pl.ds(row0, GROUP_ROWS)], xbuf, sem.at[0])
            cp.start()
            cp.wait()

        wgb[...] = wg_ref[0].astype(BF16)
        wub[...] = wu_ref[0].astype(BF16)
        wdb[...] = wd_ref[0].astype(BF16)

        def blk(s, carry):
            rows = pl.ds(pl.multiple_of(s * MOE_BLOCK, MOE_BLOCK), MOE_BLOCK)
            xb = xbuf[rows, :].astype(BF16)
            gate = _dot(xb, wgb[...]) + bg_ref[0]
            up = _dot(xb, wub[...]) + bu_ref[0]
            gate = jnp.minimum(gate, SWIGLU_LIMIT)
            up = jnp.clip(up, -SWIGLU_LIMIT, SWIGLU_LIMIT)
            act = (up + 1.0) * (gate * jax.nn.sigmoid(SWIGLU_ALPHA * gate))
            part = _dot(act.astype(BF16), wdb[...])

            @pl.when(j == 0)
            def _():
                acc[rows, :] = part + bd_ref[0]

            @pl.when(j > 0)
            def _():
                acc[rows, :] = acc[rows, :] + part

            return carry

        lax.fori_loop(0, nblk, blk, 0)

        @pl.when(j == pl.num_programs(1) - 1)
        def _():
            def out_copy(s):
                off = pl.multiple_of(s * MOE_BLOCK, MOE_BLOCK)
                return pltpu.make_async_copy(acc.at[pl.ds(off, MOE_BLOCK)],
                                             ys_ref.at[pl.ds(row0 + off, MOE_BLOCK)], sem.at[1])

            def start(s, carry):
                out_copy(s).start()
                return carry

            def wait(s, carry):
                out_copy(s).wait()
                return carry

            lax.fori_loop(0, nblk, start, 0)
            lax.fori_loop(0, nblk, wait, 0)


def _experts(ge, gs, gn, xs, w_gu, b_gu, w_dn, b_dn):
    R, D = xs.shape
    E = w_gu.shape[0]
    G = ge.shape[0]
    J = D_FF // FF_TILE

    def jj(g, j, gn_ref):
        return jnp.where(gn_ref[g] > 0, j, J - 1)

    ys0 = jnp.zeros((R, D), F32)
    return pl.pallas_call(
        _experts_kernel,
        grid_spec=pltpu.PrefetchScalarGridSpec(
            num_scalar_prefetch=3,
            grid=(G, J),
            in_specs=[pl.BlockSpec(memory_space=pl.ANY),
                      pl.BlockSpec((1, D, FF_TILE), lambda g, j, ge, gs, gn: (ge[g], 0, jj(g, j, gn))),
                      pl.BlockSpec((1, D, FF_TILE), lambda g, j, ge, gs, gn: (ge[g], 0, J + jj(g, j, gn))),
                      pl.BlockSpec((1, FF_TILE, D), lambda g, j, ge, gs, gn: (ge[g], jj(g, j, gn), 0)),
                      pl.BlockSpec((1, 1, FF_TILE), lambda g, j, ge, gs, gn: (ge[g], 0, jj(g, j, gn))),
                      pl.BlockSpec((1, 1, FF_TILE), lambda g, j, ge, gs, gn: (ge[g], 0, J + jj(g, j, gn))),
                      pl.BlockSpec((1, 1, D), lambda g, j, ge, gs, gn: (ge[g], 0, 0)),
                      pl.BlockSpec(memory_space=pl.ANY)],
            out_specs=pl.BlockSpec(memory_space=pl.ANY),
            scratch_shapes=[pltpu.VMEM((GROUP_ROWS, D), F32),
                            pltpu.VMEM((GROUP_ROWS, D), F32),
                            pltpu.VMEM((D, FF_TILE), BF16),
                            pltpu.VMEM((D, FF_TILE), BF16),
                            pltpu.VMEM((FF_TILE, D), BF16),
                            pltpu.SemaphoreType.DMA((2,))]),
        out_shape=jax.ShapeDtypeStruct((R, D), F32),
        input_output_aliases={10: 0},
        compiler_params=_cparams(("arbitrary", "arbitrary")),
        name="experts",
    )(ge, gs, gn, xs, w_gu, w_gu, w_dn, b_gu.reshape(E, 1, 2 * D_FF), b_gu.reshape(E, 1, 2 * D_FF),
      b_dn.reshape(E, 1, D), ys0)


def _combine_kernel(dest_ref, ys_ref, x1_ref, gate_ref, gt_ref, fg_ref, o_ref, ybuf, sem):
    i = pl.program_id(0)
    tm = x1_ref.shape[0]
    n_tok = pl.num_programs(0) * tm

    def copy(t, j):
        d = dest_ref[j * n_tok + i * tm + t]
        return pltpu.make_async_copy(ys_ref.at[pl.ds(d, 1)], ybuf.at[j, pl.ds(t, 1)], sem)

    def issue(t, carry):
        for j in range(TOP_K):
            copy(t, j).start()
        return carry

    lax.fori_loop(0, tm, issue, 0)

    def drain(t, carry):
        for j in range(TOP_K):
            copy(t, j).wait()
        return carry

    lax.fori_loop(0, tm, drain, 0)

    gts = gate_ref[...]
    y = ybuf[0] * gts[:, 0:1]
    for j in range(1, TOP_K):
        y = y + ybuf[j] * gts[:, j:j + 1]
    x2 = x1_ref[...] + gt_ref[0] * y
    o_ref[...] = (x2 * lax.rsqrt(jnp.mean(x2 * x2, axis=-1, keepdims=True) + RMS_EPS)) * fg_ref[...]


def _combine(dest_flat, ys, x1, gate_cols, gt, final_g, seq):
    N, D = x1.shape
    tm = 256
    tiles_per_b = seq // tm
    return pl.pallas_call(
        _combine_kernel,
        grid_spec=pltpu.PrefetchScalarGridSpec(
            num_scalar_prefetch=1,
            grid=(N // tm,),
            in_specs=[pl.BlockSpec(memory_space=pl.ANY),
                      pl.BlockSpec((tm, D), lambda i, d: (i, 0)),
                      pl.BlockSpec((tm, 128), lambda i, d: (i, 0)),
                      pl.BlockSpec((1, 1, D), lambda i, d: (i // tiles_per_b, 0, 0)),
                      pl.BlockSpec((1, D), lambda i, d: (0, 0))],
            out_specs=pl.BlockSpec((tm, D), lambda i, d: (i, 0)),
            scratch_shapes=[pltpu.VMEM((TOP_K, tm, D), F32),
                            pltpu.SemaphoreType.DMA(())]),
        out_shape=jax.ShapeDtypeStruct((N, D), F32),
        compiler_params=_cparams(("arbitrary",)),
        name="combine",
    )(dest_flat, ys, x1, gate_cols, gt, final_g.reshape(1, D))


def _relayout_w_in(w_in):
    D = w_in.shape[0]
    o = 0
    seg = {}
    for name, n in (("r", 1024), ("wl", 64), ("k", 1024), ("v", 1024), ("al", 64), ("gl", 160),
                    ("q", 1024), ("mk", 1024), ("mv", 1024), ("mo", 1024), ("i", 4), ("f", 4)):
        seg[name] = w_in[:, o:o + n]
        o += n
    z = lambda n: jnp.zeros((D, n), w_in.dtype)
    return jnp.concatenate(
        [seg["r"], seg["k"], seg["v"], seg["q"], seg["mk"], seg["mv"], seg["mo"],
         seg["gl"], z(96), seg["wl"], seg["al"], seg["i"], seg["f"], z(120)], axis=1).astype(BF16)


def kernel(x, c, ada_w, ada_b, norm1_g, w_in, rwkv_mu, rwkv_w0, rwkv_w2, rwkv_a0, rwkv_a2, rwkv_g2, rwkv_kk, rwkv_ka, rwkv_rk, rwkv_ln_w, rwkv_ln_b, mlstm_conv_w, mlstm_conv_b, mlstm_b_i, mlstm_b_f, mlstm_norm_g, w_out, norm2_g, router_w, router_b, moe_w_gu, moe_b_gu, moe_w_dn, moe_b_dn, final_g):
    B, T, D = x.shape
    N = B * T
    x2 = x.reshape(N, D)
    l = 0

    mod = _adaln(c, ada_w[l], ada_b[l])
    sh_m, sc_m, gt_m, sh_f, sc_f, gt_f = [m.reshape(B, 1, D) for m in jnp.split(mod, 6, axis=-1)]

    proj = _in_proj(x2, norm1_g[l], sc_m, sh_m, _relayout_w_in(w_in[l]), T)

    mu = rwkv_mu[l]
    mu_r, mu_wl, mu_k, mu_v, mu_al, mu_gl = (mu[0:1024], mu[1024:1088], mu[1088:2112], mu[2112:3136],
                                             mu[3136:3200], mu[3200:3360])
    row = lambda a: a.reshape(1, -1)
    zrows = lambda n: jnp.zeros((n, RWKV_DIM), F32)
    rw = {
        "mu_r": row(mu_r), "mu_k": row(mu_k), "mu_v": row(mu_v),
        "mu_gl": row(jnp.concatenate([mu_gl, jnp.zeros((96,), F32)])),
        "mu_lo": row(jnp.concatenate([mu_wl, mu_al])),
        "w0": row(rwkv_w0[l]), "a0": row(rwkv_a0[l]), "kk": row(rwkv_kk[l]), "ka": row(rwkv_ka[l]),
        "rk": row(rwkv_rk[l]), "ln_w": row(rwkv_ln_w[l]), "ln_b": row(rwkv_ln_b[l]),
        "w2": jnp.concatenate([rwkv_w2[l], zrows(A_LORA)], axis=0),
        "a2": jnp.concatenate([zrows(DECAY_LORA), rwkv_a2[l]], axis=0),
        "g2": jnp.concatenate([rwkv_g2[l], zrows(256 - G_LORA)], axis=0).astype(BF16),
    }
    y_rwkv = _rwkv(proj, B, T, rw)

    cw = mlstm_conv_w[l]
    cb = mlstm_conv_b[l]
    ml = {
        "cw_q": cw[:, :MLSTM_DIM], "cw_k": cw[:, MLSTM_DIM:],
        "cb_q": row(cb[:MLSTM_DIM]), "cb_k": row(cb[MLSTM_DIM:]),
        "gbias": row(jnp.concatenate([mlstm_b_i[l], mlstm_b_f[l], jnp.zeros((120,), F32)])),
        "norm_g": row(mlstm_norm_g[l]),
    }
    y_mlstm = _mlstm(proj, B, T, ml)

    rw_p = jnp.concatenate([router_w[l], jnp.zeros((D, 128 - N_EXPERTS), F32)], axis=1)
    rb_p = jnp.concatenate([router_b[l], jnp.zeros((128 - N_EXPERTS,), F32)]).reshape(1, 128)
    x1, h2, idx, gate_cols, rank, cnt = _out_route(
        y_rwkv, y_mlstm, x2, w_out[l].astype(BF16), gt_m, norm2_g[l], sc_f, sh_f, rw_p, rb_p, T)

    dest = _dest(idx, rank, cnt).reshape(-1)

    counts = cnt[:, 0]
    per = GROUP_ROWS // MOE_BLOCK
    nblk = (counts + MOE_BLOCK - 1) // MOE_BLOCK
    blk_start = jnp.cumsum(nblk) - nblk
    ngrp = (nblk + per - 1) // per
    gend = jnp.cumsum(ngrp)
    gstart = gend - ngrp
    n_groups = N_EXPERTS + (N * TOP_K) // GROUP_ROWS
    gid = jnp.arange(n_groups, dtype=jnp.int32)
    ge_raw = jnp.minimum(jnp.searchsorted(gend, gid, side="right"), N_EXPERTS - 1).astype(jnp.int32)
    valid = gid < gend[-1]
    last_e = jnp.minimum(jnp.searchsorted(gend, jnp.maximum(gend[-1] - 1, 0), side="right"),
                         N_EXPERTS - 1).astype(jnp.int32)
    ge = jnp.where(valid, ge_raw, last_e).astype(jnp.int32)
    within = gid - gstart[ge_raw]
    gs = jnp.where(valid, blk_start[ge_raw] + within * per, 0).astype(jnp.int32)
    gn = jnp.where(valid, jnp.minimum(nblk[ge_raw] - within * per, per), 0).astype(jnp.int32)

    n_rows = ((N * TOP_K) // MOE_BLOCK + N_EXPERTS) * MOE_BLOCK + GROUP_ROWS
    xs = _dispatch(dest, h2, n_rows)
    ys = _experts(ge, gs, gn, xs, moe_w_gu[l], moe_b_gu[l], moe_w_dn[l], moe_b_dn[l])
    out = _combine(dest, ys, x1, gate_cols, gt_f, final_g, T)
    return out.reshape(B, T, D)
```

```python
import functools

import jax
import jax.numpy as jnp
from jax import lax
from jax.experimental import pallas as pl
from jax.experimental.pallas import tpu as pltpu

F32 = jnp.float32
BF16 = jnp.bfloat16
HIGHEST = lax.Precision.HIGHEST

D_MODEL = 2048
CHUNK = 64
RMS_EPS = 1e-5
RWKV_DIM = 1024
RWKV_HEAD = 64
DECAY_LORA = 64
A_LORA = 64
G_LORA = 160
GN_EPS = 64e-5
MLSTM_DIM = 1024
MLSTM_HEADS = 4
MLSTM_HEAD = 256
CONV_W = 4
HEAD_NORM_EPS = 1e-6
N_EXPERTS = 32
TOP_K = 4
D_FF = 2048
SWIGLU_LIMIT = 7.0
SWIGLU_ALPHA = 1.702
MOE_BLOCK = 256

COL_R, COL_K, COL_V, COL_Q, COL_MK, COL_MV, COL_MO = (i * 1024 for i in range(7))
COL_GL = 7168
COL_LORA = 7424
COL_IF = 7552
IN_COLS_P = 7680
QUAD = 256

VMEM_LIMIT = 56 * 1024 * 1024


def _cparams(sem, vmem=VMEM_LIMIT):
    return pltpu.CompilerParams(dimension_semantics=sem, vmem_limit_bytes=vmem)


def _dot(a, b, **kw):
    return jnp.dot(a, b, preferred_element_type=F32, **kw)


def _dot_nt(a, b, **kw):
    return lax.dot_general(a, b, (((1,), (1,)), ((), ())), preferred_element_type=F32, **kw)


def _dot_tn(a, b, **kw):
    return lax.dot_general(a, b, (((0,), (0,)), ((), ())), preferred_element_type=F32, **kw)


def _split3(x):
    h = x.astype(BF16)
    r1 = x - h.astype(F32)
    m = r1.astype(BF16)
    l = (r1 - m.astype(F32)).astype(BF16)
    return h, m, l


def _iota(shape, dim):
    return lax.broadcasted_iota(jnp.int32, shape, dim)


def _adaln_kernel(c_ref, w_ref, b_ref, o_ref):
    c = c_ref[...]
    s = c * jax.nn.sigmoid(c)
    o_ref[...] = _dot(s, w_ref[...], precision=HIGHEST) + b_ref[...]


def _adaln(c, ada_w, ada_b):
    B, D = c.shape
    n_out = ada_w.shape[1]
    tn = 1024
    c8 = jnp.zeros((8, D), F32).at[:B].set(c)
    out = pl.pallas_call(
        _adaln_kernel,
        grid=(n_out // tn,),
        in_specs=[pl.BlockSpec((8, D), lambda j: (0, 0)),
                  pl.BlockSpec((D, tn), lambda j: (0, j)),
                  pl.BlockSpec((1, tn), lambda j: (0, j))],
        out_specs=pl.BlockSpec((8, tn), lambda j: (0, j)),
        out_shape=jax.ShapeDtypeStruct((8, n_out), F32),
        compiler_params=_cparams(("parallel",)),
        name="adaln",
    )(c8, ada_w, ada_b.reshape(1, n_out))
    return out[:B]


def _inproj_kernel(x_ref, g_ref, sc_ref, sh_ref, w_ref, o_ref, h_scr):
    @pl.when(pl.program_id(1) == 0)
    def _():
        x = x_ref[...]
        y = x * lax.rsqrt(jnp.mean(x * x, axis=-1, keepdims=True) + RMS_EPS)
        h = (y * g_ref[...]) * (1.0 + sc_ref[0]) + sh_ref[0]
        h_scr[...] = h.astype(BF16)

    o_ref[...] = _dot(h_scr[...], w_ref[...])


def _in_proj(x2, norm_g, sc, sh, w_p, seq):
    N, D = x2.shape
    tm, tn = 1024, 768
    tiles_per_b = seq // tm
    ncols = w_p.shape[1]
    return pl.pallas_call(
        _inproj_kernel,
        grid=(N // tm, ncols // tn),
        in_specs=[pl.BlockSpec((tm, D), lambda i, j: (i, 0)),
                  pl.BlockSpec((1, D), lambda i, j: (0, 0)),
                  pl.BlockSpec((1, 1, D), lambda i, j: (i // tiles_per_b, 0, 0)),
                  pl.BlockSpec((1, 1, D), lambda i, j: (i // tiles_per_b, 0, 0)),
                  pl.BlockSpec((D, tn), lambda i, j: (0, j))],
        out_specs=pl.BlockSpec((tm, tn), lambda i, j: (i, j)),
        out_shape=jax.ShapeDtypeStruct((N, ncols), F32),
        scratch_shapes=[pltpu.VMEM((tm, D), BF16)],
        compiler_params=_cparams(("parallel", "arbitrary")),
        name="in_proj",
    )(x2, norm_g.reshape(1, D), sc, sh, w_p)


def _shift_rows(x, prev_row):
    rolled = pltpu.roll(x, 1, axis=0)
    row = _iota(x.shape, 0)
    return jnp.where(row == 0, prev_row, rolled)


def _rwkv_kernel(r_ref, k_ref, v_ref, gl_ref, lo_ref,
                 mu_r_ref, mu_k_ref, mu_v_ref, mu_gl_ref, mu_lo_ref,
                 w0_ref, a0_ref, kkp_ref, kap_ref, rk_ref, lnw_ref, lnb_ref,
                 w2_ref, a2_ref, g2_ref,
                 o_ref,
                 s_scr, prev_scr):
    L = CHUNK
    c = pl.program_id(1)

    @pl.when(c == 0)
    def _():
        s_scr[...] = jnp.zeros_like(s_scr)
        prev_scr[...] = jnp.zeros_like(prev_scr)

    def lerp(ref, mu_ref, lo, width):
        x = ref[...]
        prev = prev_scr[0:1, lo:lo + width]
        out = x + (_shift_rows(x, prev) - x) * mu_ref[...]
        prev_scr[0:1, lo:lo + width] = x[L - 1:L, :]
        return out

    r = lerp(r_ref, mu_r_ref, 0, RWKV_DIM)
    k = lerp(k_ref, mu_k_ref, 1024, RWKV_DIM)
    v = lerp(v_ref, mu_v_ref, 2048, RWKV_DIM)
    gl = lerp(gl_ref, mu_gl_ref, 3072, 256)
    lo = lerp(lo_ref, mu_lo_ref, 3328, 128)

    w = -jax.nn.softplus(-(w0_ref[...] + _dot(jnp.tanh(lo), w2_ref[...], precision=HIGHEST))) - 0.5
    lw = -jnp.exp(w)
    a = jax.nn.sigmoid(a0_ref[...] + _dot(lo, a2_ref[...], precision=HIGHEST))
    g = _dot(jax.nn.sigmoid(gl).astype(BF16), g2_ref[...])
    kkraw = k * kkp_ref[...]
    k2 = k * (1.0 + (a - 1.0) * kap_ref[...])

    ltri = jnp.where(_iota((L, L), 1) <= _iota((L, L), 0), 1.0, 0.0).astype(BF16)
    qr = _iota((QUAD, QUAD), 0)
    qc = _iota((QUAD, QUAD), 1)
    bd_mask = (qr // RWKV_HEAD) == (qc // RWKV_HEAD)
    bd = jnp.where(bd_mask, 1.0, 0.0).astype(BF16)
    strict_bd = bd_mask & ((qc % L) < (qr % L))
    incl_bd = bd_mask & ((qc % L) <= (qr % L))
    lane_head = _iota((1, QUAD), 1) // RWKV_HEAD
    n_heads = QUAD // RWKV_HEAD
    quads = range(RWKV_DIM // QUAD)

    def head_sum(x):
        h, m, l = _split3(x)
        return _dot(h, bd) + _dot(m, bd) + _dot(l, bd)

    def time_cumsum(x):
        h, m, l = _split3(x)
        return _dot(ltri, h) + _dot(ltri, m) + _dot(ltri, l)

    def stack_heads(x):
        return jnp.concatenate([jnp.where(lane_head == h, x, jnp.zeros_like(x)) for h in range(n_heads)], axis=0)

    def unstack_heads(x):
        out = x[0:L]
        for h in range(1, n_heads):
            out = out + x[h * L:(h + 1) * L]
        return out

    sls = [slice(q * QUAD, (q + 1) * QUAD) for q in quads]

    kk = []
    for sl in sls:
        kkr = kkraw[:, sl]
        kk.append(kkr / jnp.maximum(jnp.sqrt(head_sum(kkr * kkr)), 1e-12))
    cl = [time_cumsum(lw[:, sl]) for sl in sls]

    at, rt, bt, kt, p_last = [], [], [], [], []
    for q, sl in enumerate(sls):
        p_t = jnp.exp(cl[q])
        p_prev = jnp.exp(cl[q] - lw[:, sl])
        p_inv = jnp.exp(-cl[q])
        at.append((-kk[q] * p_prev).astype(BF16))
        rt.append((r[:, sl] * p_t).astype(BF16))
        bt.append((kk[q] * a[:, sl] * p_inv).astype(BF16))
        kt.append((k2[:, sl] * p_inv).astype(BF16))
        p_last.append(p_t[L - 1:L, :])

    s0 = [s_scr[q] for q in quads]
    n_pow, a_ak, a_rb, a_rk, rb4, rk4, vst, xs_st, y_state = [], [], [], [], [], [], [], [], []
    for q, sl in enumerate(sls):
        lst = jnp.concatenate([stack_heads(at[q]), stack_heads(rt[q])], axis=0)
        rb4.append(jnp.concatenate([bt[q]] * n_heads, axis=0))
        rk4.append(jnp.concatenate([kt[q]] * n_heads, axis=0))
        gb = _dot_nt(lst, rb4[q])
        gk = _dot_nt(lst, rk4[q])
        n_pow.append(jnp.where(strict_bd, gb[:QUAD], 0.0))
        a_rb.append(jnp.where(incl_bd, gb[QUAD:], 0.0).astype(BF16))
        a_ak.append(jnp.where(strict_bd, gk[:QUAD], 0.0).astype(BF16))
        a_rk.append(jnp.where(incl_bd, gk[QUAD:], 0.0).astype(BF16))
        vst.append(stack_heads(v[:, sl].astype(BF16)))
        xy = _dot_nt(jnp.concatenate([at[q], rt[q]], axis=0), s0[q].astype(BF16))
        xs_st.append(stack_heads(xy[:L]))
        y_state.append(xy[L:])

    u = [xs_st[q] + _dot(a_ak[q], vst[q]) for q in quads]
    for j in range(6):
        nb = [n_pow[q].astype(BF16) for q in quads]
        u = [u[q] + _dot(nb[q], u[q].astype(BF16)) for q in quads]
        if j < 5:
            n_pow = [_dot(nb[q], nb[q]) for q in quads]

    ub = [u[q].astype(BF16) for q in quads]
    y_st = [_dot(a_rb[q], ub[q]) + _dot(a_rk[q], vst[q]) for q in quads]
    for q in quads:
        z = s0[q] + _dot_tn(ub[q], rb4[q]) + _dot_tn(vst[q], rk4[q])
        s_scr[q] = jnp.where(bd_mask, z, 0.0) * p_last[q]

    inv_n = 1.0 / RWKV_HEAD
    y = [y_state[q] + unstack_heads(y_st[q]) for q in quads]
    mean = [head_sum(y[q]) * inv_n for q in quads]
    d = [y[q] - mean[q] for q in quads]
    var = [head_sum(d[q] * d[q]) * inv_n for q in quads]
    bonus = [head_sum(r[:, sl] * k2[:, sl] * rk_ref[:, sl]) * v[:, sl] for sl in sls]
    for q, sl in enumerate(sls):
        yn = d[q] * lax.rsqrt(var[q] + GN_EPS) * lnw_ref[:, sl] + lnb_ref[:, sl]
        o_ref[:, sl] = ((yn + bonus[q]) * g[:, sl]).astype(o_ref.dtype)


def _rwkv(proj, B, T, pr):
    NC = T // CHUNK
    L = CHUNK
    row = lambda b, c: b * NC + c

    def pspec(width, colblk):
        return pl.BlockSpec((L, width), lambda b, c: (row(b, c), colblk))

    def vec(width):
        return pl.BlockSpec((1, width), lambda b, c: (0, 0))

    def mat(rows):
        return pl.BlockSpec((rows, RWKV_DIM), lambda b, c: (0, 0))

    return pl.pallas_call(
        _rwkv_kernel,
        grid=(B, NC),
        in_specs=[pspec(1024, COL_R // 1024), pspec(1024, COL_K // 1024), pspec(1024, COL_V // 1024),
                  pspec(256, COL_GL // 256), pspec(128, COL_LORA // 128),
                  vec(1024), vec(1024), vec(1024), vec(256), vec(128),
                  vec(1024), vec(1024), vec(1024), vec(1024), vec(1024), vec(1024), vec(1024),
                  mat(128), mat(128), mat(256)],
        out_specs=pl.BlockSpec((L, RWKV_DIM), lambda b, c: (row(b, c), 0)),
        out_shape=jax.ShapeDtypeStruct((B * T, RWKV_DIM), BF16),
        scratch_shapes=[pltpu.VMEM((RWKV_DIM // QUAD, QUAD, QUAD), F32),
                        pltpu.VMEM((8, 3072 + 256 + 128), F32)],
        compiler_params=_cparams(("parallel", "arbitrary")),
        name="rwkv",
    )(proj, proj, proj, proj, proj,
      pr["mu_r"], pr["mu_k"], pr["mu_v"], pr["mu_gl"], pr["mu_lo"],
      pr["w0"], pr["a0"], pr["kk"], pr["ka"], pr["rk"], pr["ln_w"], pr["ln_b"],
      pr["w2"], pr["a2"], pr["g2"])


def _mlstm_kernel(q_ref, k_ref, v_ref, o_ref, gt_ref,
                  cwq_ref, cwk_ref, cbq_ref, cbk_ref, gbias_ref, ng_ref,
                  out_ref,
                  c_scr, n_scr, m_scr, prevq_scr, prevk_scr):
    L = CHUNK
    c = pl.program_id(1)

    @pl.when(c == 0)
    def _():
        c_scr[...] = jnp.zeros_like(c_scr)
        n_scr[...] = jnp.zeros_like(n_scr)
        m_scr[...] = jnp.zeros_like(m_scr)
        prevq_scr[...] = jnp.zeros_like(prevq_scr)
        prevk_scr[...] = jnp.zeros_like(prevk_scr)

    def conv_silu(ref, prev_scr, cw_ref, cb_ref):
        x = ref[...]
        prev8 = prev_scr[...]
        row8 = _iota(prev8.shape, 0)
        out = cb_ref[...] + x * cw_ref[CONV_W - 1:CONV_W, :]
        for d in range(1, CONV_W):
            rolled = pltpu.roll(x, d, axis=0)
            first8 = jnp.where(row8 < d, pltpu.roll(prev8, d, axis=0), rolled[:8])
            xs = jnp.concatenate([first8, rolled[8:]], axis=0)
            out = out + xs * cw_ref[CONV_W - 1 - d:CONV_W - d, :]
        prev_scr[...] = x[L - 8:L, :]
        return out * jax.nn.sigmoid(out)

    qf = conv_silu(q_ref, prevq_scr, cwq_ref, cbq_ref)
    kf = conv_silu(k_ref, prevk_scr, cwk_ref, cbk_ref) * (MLSTM_HEAD ** -0.5)
    vf = v_ref[...]

    z = gt_ref[...] + gbias_ref[...]
    lane = _iota(z.shape, 1)
    gc = jnp.where(lane < MLSTM_HEADS, z,
                   jnp.where(lane < 2 * MLSTM_HEADS, jax.nn.log_sigmoid(z), 0.0))
    row = _iota((L, L), 0)
    col = _iota((L, L), 1)
    causal = col <= row
    ltri = jnp.where(causal, 1.0, 0.0).astype(BF16)
    gh, gm, glo = _split3(gc)
    cum = _dot(ltri, gh) + _dot(ltri, gm) + _dot(ltri, glo)
    gc_t = gc.T
    cum_t = cum.T

    for h in range(MLSTM_HEADS):
        sl = slice(h * MLSTM_HEAD, (h + 1) * MLSTM_HEAD)
        qh, kh, vh = qf[:, sl], kf[:, sl], vf[:, sl]
        li_row = gc_t[h:h + 1, :]
        li_col = gc[:, h:h + 1]
        b_row = cum_t[MLSTM_HEADS + h:MLSTM_HEADS + h + 1, :]
        b_col = cum[:, MLSTM_HEADS + h:MLSTM_HEADS + h + 1]
        m_prev = m_scr[h:h + 1, 0:1]
        c_prev = c_scr[h]
        n_prev = n_scr[h:h + 1, :]

        a_inter = b_col + m_prev
        dm = jnp.where(causal, b_col - b_row + li_row, -jnp.inf)
        m_t = jnp.maximum(a_inter, jnp.max(dm, axis=-1, keepdims=True))
        w_inter = jnp.exp(a_inter - m_t)
        wmat = jnp.exp(dm - m_t)
        qb, kb, vb = qh.astype(BF16), kh.astype(BF16), vh.astype(BF16)
        s = _dot_nt(qb, kb) * wmat
        num = w_inter * _dot_nt(qb, c_prev.astype(BF16)) + _dot(s.astype(BF16), vb)
        den = w_inter * jnp.sum(qh * n_prev, axis=-1, keepdims=True) + jnp.sum(s, axis=-1, keepdims=True)
        hh = num / jnp.maximum(jnp.abs(den), jnp.exp(-m_t))

        m_new = m_t[L - 1:L, :]
        b_last = b_col[L - 1:L, :]
        g_state = jnp.exp(b_last + m_prev - m_new)
        w_s = jnp.exp(b_last - b_col + li_col - m_new)
        c_scr[h] = g_state * c_prev + _dot_tn((vh * w_s).astype(BF16), kb)
        n_scr[h:h + 1, :] = g_state * n_prev + jnp.sum(kh * w_s, axis=0, keepdims=True)
        m_scr[h:h + 1, :] = jnp.broadcast_to(m_new, (1, m_scr.shape[1]))

        hn = hh * lax.rsqrt(jnp.mean(hh * hh, axis=-1, keepdims=True) + HEAD_NORM_EPS)
        out_ref[:, sl] = ((hn * ng_ref[:, sl]) * jax.nn.sigmoid(o_ref[:, sl])).astype(out_ref.dtype)


def _mlstm(proj, B, T, pr):
    NC = T // CHUNK
    L = CHUNK
    row = lambda b, c: b * NC + c

    def pspec(width, colblk):
        return pl.BlockSpec((L, width), lambda b, c: (row(b, c), colblk))

    def cst(shape):
        return pl.BlockSpec(shape, lambda b, c: (0, 0))

    return pl.pallas_call(
        _mlstm_kernel,
        grid=(B, NC),
        in_specs=[pspec(1024, COL_Q // 1024), pspec(1024, COL_MK // 1024), pspec(1024, COL_MV // 1024),
                  pspec(1024, COL_MO // 1024), pspec(128, COL_IF // 128),
                  cst((CONV_W, 1024)), cst((CONV_W, 1024)), cst((1, 1024)), cst((1, 1024)),
                  cst((1, 128)), cst((1, 1024))],
        out_specs=pl.BlockSpec((L, MLSTM_DIM), lambda b, c: (row(b, c), 0)),
        out_shape=jax.ShapeDtypeStruct((B * T, MLSTM_DIM), BF16),
        scratch_shapes=[pltpu.VMEM((MLSTM_HEADS, MLSTM_HEAD, MLSTM_HEAD), F32),
                        pltpu.VMEM((8, MLSTM_HEAD), F32),
                        pltpu.VMEM((8, 128), F32),
                        pltpu.VMEM((8, 1024), F32),
                        pltpu.VMEM((8, 1024), F32)],
        compiler_params=_cparams(("parallel", "arbitrary")),
        name="mlstm",
    )(proj, proj, proj, proj, proj,
      pr["cw_q"], pr["cw_k"], pr["cb_q"], pr["cb_k"], pr["gbias"], pr["norm_g"])


def _out_route_kernel(yr_ref, ym_ref, x_ref, wo_ref, gt_ref, g2_ref, sc_ref, sh_ref, rw_ref, rb_ref,
                      x1_ref, h2_ref, idx_ref, gate_ref, rank_ref, cnt_ref,
                      carry_scr):
    i = pl.program_id(0)
    tm = x_ref.shape[0]

    @pl.when(i == 0)
    def _():
        carry_scr[...] = jnp.zeros_like(carry_scr)

    mix = _dot(yr_ref[...], wo_ref[0:RWKV_DIM, :]) + _dot(ym_ref[...], wo_ref[RWKV_DIM:, :])
    x1 = x_ref[...] + gt_ref[0] * mix
    x1_ref[...] = x1
    y = x1 * lax.rsqrt(jnp.mean(x1 * x1, axis=-1, keepdims=True) + RMS_EPS)
    h2 = (y * g2_ref[...]) * (1.0 + sc_ref[0]) + sh_ref[0]
    h2_ref[...] = h2

    logits = _dot(h2, rw_ref[...], precision=HIGHEST) + rb_ref[...]
    lt = logits.T[:N_EXPERTS, :]
    e_iota = _iota(lt.shape, 0)
    onehots, vals, idxs = [], [], []
    for _ in range(TOP_K):
        mx = jnp.max(lt, axis=0, keepdims=True)
        idx = jnp.min(jnp.where(lt == mx, e_iota, N_EXPERTS), axis=0, keepdims=True)
        sel = e_iota == idx
        onehots.append(sel)
        vals.append(mx)
        idxs.append(idx)
        lt = jnp.where(sel, -jnp.inf, lt)
    exps = [jnp.exp(vv - vals[0]) for vv in vals]
    denom = exps[0] + exps[1] + exps[2] + exps[3]
    gates = [e / denom for e in exps]

    member = jnp.zeros(onehots[0].shape, F32)
    for sel in onehots:
        member = member + jnp.where(sel, 1.0, 0.0)
    ur = _iota((tm, tm), 0)
    uc = _iota((tm, tm), 1)
    ustrict = jnp.where(ur < uc, 1.0, 0.0).astype(BF16)
    before = _dot(member.astype(BF16), ustrict) + carry_scr[:, 0:1]
    for j in range(TOP_K):
        rank = jnp.sum(jnp.where(onehots[j], before, 0.0), axis=0, keepdims=True)
        rank_ref[j:j + 1, :] = rank.astype(jnp.int32)
        idx_ref[j:j + 1, :] = idxs[j]
    new_carry = carry_scr[...] + jnp.sum(member, axis=1, keepdims=True)
    carry_scr[...] = new_carry
    cnt_ref[...] = new_carry.astype(jnp.int32)

    grows = jnp.concatenate(gates + [jnp.zeros((128 - TOP_K, tm), F32)], axis=0)
    gate_ref[...] = grows.T


def _out_route(yr, ym, x2, w_out_b, gt, g2, sc, sh, rw_p, rb_p, seq):
    N, D = x2.shape
    tm = 512
    tiles_per_b = seq // tm
    bvec = pl.BlockSpec((1, 1, D), lambda i: (i // tiles_per_b, 0, 0))
    return pl.pallas_call(
        _out_route_kernel,
        grid=(N // tm,),
        in_specs=[pl.BlockSpec((tm, RWKV_DIM), lambda i: (i, 0)),
                  pl.BlockSpec((tm, MLSTM_DIM), lambda i: (i, 0)),
                  pl.BlockSpec((tm, D), lambda i: (i, 0)),
                  pl.BlockSpec((D, D), lambda i: (0, 0)),
                  bvec,
                  pl.BlockSpec((1, D), lambda i: (0, 0)),
                  bvec, bvec,
                  pl.BlockSpec((D, 128), lambda i: (0, 0)),
                  pl.BlockSpec((1, 128), lambda i: (0, 0))],
        out_specs=[pl.BlockSpec((tm, D), lambda i: (i, 0)),
                   pl.BlockSpec((tm, D), lambda i: (i, 0)),
                   pl.BlockSpec((TOP_K, tm), lambda i: (0, i)),
                   pl.BlockSpec((tm, 128), lambda i: (i, 0)),
                   pl.BlockSpec((TOP_K, tm), lambda i: (0, i)),
                   pl.BlockSpec((N_EXPERTS, 128), lambda i: (0, 0))],
        out_shape=[jax.ShapeDtypeStruct((N, D), F32),
                   jax.ShapeDtypeStruct((N, D), F32),
                   jax.ShapeDtypeStruct((TOP_K, N), jnp.int32),
                   jax.ShapeDtypeStruct((N, 128), F32),
                   jax.ShapeDtypeStruct((TOP_K, N), jnp.int32),
                   jax.ShapeDtypeStruct((N_EXPERTS, 128), jnp.int32)],
        scratch_shapes=[pltpu.VMEM((N_EXPERTS, 128), F32)],
        compiler_params=_cparams(("arbitrary",)),
        name="out_route",
    )(yr, ym, x2, w_out_b, gt, g2.reshape(1, D), sc, sh, rw_p, rb_p)


def _dest_kernel(idx_ref, rank_ref, cnt_ref, dest_ref):
    cnt = cnt_ref[:, 0:1].astype(F32)
    padded = jnp.floor((cnt + (MOE_BLOCK - 1)) * (1.0 / MOE_BLOCK)) * MOE_BLOCK
    er = _iota((N_EXPERTS, N_EXPERTS), 0)
    ec = _iota((N_EXPERTS, N_EXPERTS), 1)
    before = jnp.where(ec < er, 1.0, 0.0)
    pstart = _dot(before, jnp.broadcast_to(padded, (N_EXPERTS, 128)), precision=HIGHEST)[:, 0:1]
    e_iota = _iota((N_EXPERTS, idx_ref.shape[1]), 0)
    for j in range(TOP_K):
        sel = e_iota == idx_ref[j:j + 1, :]
        base = jnp.sum(jnp.where(sel, pstart, 0.0), axis=0, keepdims=True)
        dest_ref[j:j + 1, :] = base.astype(jnp.int32) + rank_ref[j:j + 1, :]


def _dest(idx, rank, cnt):
    N = idx.shape[1]
    tn = 2048
    return pl.pallas_call(
        _dest_kernel,
        grid=(N // tn,),
        in_specs=[pl.BlockSpec((TOP_K, tn), lambda i: (0, i)),
                  pl.BlockSpec((TOP_K, tn), lambda i: (0, i)),
                  pl.BlockSpec((N_EXPERTS, 128), lambda i: (0, 0))],
        out_specs=pl.BlockSpec((TOP_K, tn), lambda i: (0, i)),
        out_shape=jax.ShapeDtypeStruct((TOP_K, N), jnp.int32),
        compiler_params=_cparams(("parallel",)),
        name="dest",
    )(idx, rank, cnt)


def _dispatch_kernel(dest_ref, h_ref, xs_in_ref, xs_ref, sem):
    del xs_in_ref
    i = pl.program_id(0)
    tm = h_ref.shape[0]
    n_tok = pl.num_programs(0) * tm

    def copy(t, j):
        d = dest_ref[j * n_tok + i * tm + t]
        return pltpu.make_async_copy(h_ref.at[pl.ds(t, 1)], xs_ref.at[pl.ds(d, 1)], sem)

    def issue(t, carry):
        for j in range(TOP_K):
            copy(t, j).start()
        return carry

    lax.fori_loop(0, tm, issue, 0)

    def drain(t, carry):
        for j in range(TOP_K):
            copy(t, j).wait()
        return carry

    lax.fori_loop(0, tm, drain, 0)


def _dispatch(dest_flat, h2, n_rows):
    N, D = h2.shape
    tm = 256
    xs0 = jnp.zeros((n_rows, D), h2.dtype)
    return pl.pallas_call(
        _dispatch_kernel,
        grid_spec=pltpu.PrefetchScalarGridSpec(
            num_scalar_prefetch=1,
            grid=(N // tm,),
            in_specs=[pl.BlockSpec((tm, D), lambda i, d: (i, 0)),
                      pl.BlockSpec(memory_space=pl.ANY)],
            out_specs=pl.BlockSpec(memory_space=pl.ANY),
            scratch_shapes=[pltpu.SemaphoreType.DMA(())]),
        out_shape=jax.ShapeDtypeStruct((n_rows, D), h2.dtype),
        input_output_aliases={2: 0},
        compiler_params=_cparams(("arbitrary",)),
        name="dispatch",
    )(dest_flat, h2, xs0)


GROUP_ROWS = 1024
FF_TILE = 256


def _experts_kernel(ge_ref, gs_ref, gn_ref,
                    xs_ref, wg_ref, wu_ref, wd_ref, bg_ref, bu_ref, bd_ref, ys_in_ref,
                    ys_ref,
                    xbuf, xb16, acc, wgb, wub, wdb, sem):
    del ys_in_ref
    g = pl.program_id(0)
    j = pl.program_id(1)
    nblk = gn_ref[g]
    row0 = pl.multiple_of(gs_ref[g] * MOE_BLOCK, MOE_BLOCK)

    def in_copy(s):
        off = pl.multiple_of(s * MOE_BLOCK, MOE_BLOCK)
        return pltpu.make_async_copy(xs_ref.at[pl.ds(row0 + off, MOE_BLOCK)],
                                     xbuf.at[pl.ds(off, MOE_BLOCK)], sem.at[0])

    def out_copy(s):
        off = pl.multiple_of(s * MOE_BLOCK, MOE_BLOCK)
        return pltpu.make_async_copy(acc.at[pl.ds(off, MOE_BLOCK)],
                                     ys_ref.at[pl.ds(row0 + off, MOE_BLOCK)], sem.at[1])

    def for_blocks(fn):
        def body(s, carry):
            fn(s)
            return carry
        lax.fori_loop(0, nblk, body, 0)

    @pl.when(nblk > 0)
    def _():
        @pl.when(j == 0)
        def _():
            for_blocks(lambda s: in_copy(s).start())
            for_blocks(lambda s: in_copy(s).wait())

        wgb[...] = wg_ref[0].astype(BF16)
        wub[...] = wu_ref[0].astype(BF16)
        wdb[...] = wd_ref[0].astype(BF16)

        for n in range(1, GROUP_ROWS // MOE_BLOCK + 1):
            m = n * MOE_BLOCK

            @pl.when(nblk == n)
            def _(m=m):
                @pl.when(j == 0)
                def _():
                    xb16[0:m, :] = xbuf[0:m, :].astype(BF16)
                    acc[0:m, :] = jnp.broadcast_to(bd_ref[0], (m, acc.shape[1]))

                xb = xb16[0:m, :]
                gate = _dot(xb, wgb[...]) + bg_ref[0]
                up = _dot(xb, wub[...]) + bu_ref[0]
                gate = jnp.minimum(gate, SWIGLU_LIMIT)
                up = jnp.clip(up, -SWIGLU_LIMIT, SWIGLU_LIMIT)
                act = (up + 1.0) * (gate * jax.nn.sigmoid(SWIGLU_ALPHA * gate))
                acc[0:m, :] = acc[0:m, :] + _dot(act.astype(BF16), wdb[...])

        @pl.when(j == pl.num_programs(1) - 1)
        def _():
            for_blocks(lambda s: out_copy(s).start())
            for_blocks(lambda s: out_copy(s).wait())


def _experts(ge, gs, gn, xs, w_gu, b_gu, w_dn, b_dn):
    R, D = xs.shape
    E = w_gu.shape[0]
    G = ge.shape[0]
    J = D_FF // FF_TILE

    def jj(g, j, gn_ref):
        return jnp.where(gn_ref[g] > 0, j, J - 1)

    ys0 = jnp.zeros((R, D), F32)
    return pl.pallas_call(
        _experts_kernel,
        grid_spec=pltpu.PrefetchScalarGridSpec(
            num_scalar_prefetch=3,
            grid=(G, J),
            in_specs=[pl.BlockSpec(memory_space=pl.ANY),
                      pl.BlockSpec((1, D, FF_TILE), lambda g, j, ge, gs, gn: (ge[g], 0, jj(g, j, gn))),
                      pl.BlockSpec((1, D, FF_TILE), lambda g, j, ge, gs, gn: (ge[g], 0, J + jj(g, j, gn))),
                      pl.BlockSpec((1, FF_TILE, D), lambda g, j, ge, gs, gn: (ge[g], jj(g, j, gn), 0)),
                      pl.BlockSpec((1, 1, FF_TILE), lambda g, j, ge, gs, gn: (ge[g], 0, jj(g, j, gn))),
                      pl.BlockSpec((1, 1, FF_TILE), lambda g, j, ge, gs, gn: (ge[g], 0, J + jj(g, j, gn))),
                      pl.BlockSpec((1, 1, D), lambda g, j, ge, gs, gn: (ge[g], 0, 0)),
                      pl.BlockSpec(memory_space=pl.ANY)],
            out_specs=pl.BlockSpec(memory_space=pl.ANY),
            scratch_shapes=[pltpu.VMEM((GROUP_ROWS, D), F32),
                            pltpu.VMEM((GROUP_ROWS, D), BF16),
                            pltpu.VMEM((GROUP_ROWS, D), F32),
                            pltpu.VMEM((D, FF_TILE), BF16),
                            pltpu.VMEM((D, FF_TILE), BF16),
                            pltpu.VMEM((FF_TILE, D), BF16),
                            pltpu.SemaphoreType.DMA((2,))]),
        out_shape=jax.ShapeDtypeStruct((R, D), F32),
        input_output_aliases={10: 0},
        compiler_params=_cparams(("arbitrary", "arbitrary")),
        name="experts",
    )(ge, gs, gn, xs, w_gu, w_gu, w_dn, b_gu.reshape(E, 1, 2 * D_FF), b_gu.reshape(E, 1, 2 * D_FF),
      b_dn.reshape(E, 1, D), ys0)


def _combine_kernel(dest_ref, ys_ref, x1_ref, gate_ref, gt_ref, fg_ref, o_ref, ybuf, sem):
    i = pl.program_id(0)
    tm = x1_ref.shape[0]
    n_tok = pl.num_programs(0) * tm

    def copy(t, j):
        d = dest_ref[j * n_tok + i * tm + t]
        return pltpu.make_async_copy(ys_ref.at[pl.ds(d, 1)], ybuf.at[j, pl.ds(t, 1)], sem)

    def issue(t, carry):
        for j in range(TOP_K):
            copy(t, j).start()
        return carry

    lax.fori_loop(0, tm, issue, 0)

    def drain(t, carry):
        for j in range(TOP_K):
            copy(t, j).wait()
        return carry

    lax.fori_loop(0, tm, drain, 0)

    gts = gate_ref[...]
    y = ybuf[0] * gts[:, 0:1]
    for j in range(1, TOP_K):
        y = y + ybuf[j] * gts[:, j:j + 1]
    x2 = x1_ref[...] + gt_ref[0] * y
    o_ref[...] = (x2 * lax.rsqrt(jnp.mean(x2 * x2, axis=-1, keepdims=True) + RMS_EPS)) * fg_ref[...]


def _combine(dest_flat, ys, x1, gate_cols, gt, final_g, seq):
    N, D = x1.shape
    tm = 256
    tiles_per_b = seq // tm
    return pl.pallas_call(
        _combine_kernel,
        grid_spec=pltpu.PrefetchScalarGridSpec(
            num_scalar_prefetch=1,
            grid=(N // tm,),
            in_specs=[pl.BlockSpec(memory_space=pl.ANY),
                      pl.BlockSpec((tm, D), lambda i, d: (i, 0)),
                      pl.BlockSpec((tm, 128), lambda i, d: (i, 0)),
                      pl.BlockSpec((1, 1, D), lambda i, d: (i // tiles_per_b, 0, 0)),
                      pl.BlockSpec((1, D), lambda i, d: (0, 0))],
            out_specs=pl.BlockSpec((tm, D), lambda i, d: (i, 0)),
            scratch_shapes=[pltpu.VMEM((TOP_K, tm, D), F32),
                            pltpu.SemaphoreType.DMA(())]),
        out_shape=jax.ShapeDtypeStruct((N, D), F32),
        compiler_params=_cparams(("arbitrary",)),
        name="combine",
    )(dest_flat, ys, x1, gate_cols, gt, final_g.reshape(1, D))


def _relayout_w_in(w_in):
    D = w_in.shape[0]
    o = 0
    seg = {}
    for name, n in (("r", 1024), ("wl", 64), ("k", 1024), ("v", 1024), ("al", 64), ("gl", 160),
                    ("q", 1024), ("mk", 1024), ("mv", 1024), ("mo", 1024), ("i", 4), ("f", 4)):
        seg[name] = w_in[:, o:o + n]
        o += n
    z = lambda n: jnp.zeros((D, n), w_in.dtype)
    return jnp.concatenate(
        [seg["r"], seg["k"], seg["v"], seg["q"], seg["mk"], seg["mv"], seg["mo"],
         seg["gl"], z(96), seg["wl"], seg["al"], seg["i"], seg["f"], z(120)], axis=1).astype(BF16)


def kernel(x, c, ada_w, ada_b, norm1_g, w_in, rwkv_mu, rwkv_w0, rwkv_w2, rwkv_a0, rwkv_a2, rwkv_g2, rwkv_kk, rwkv_ka, rwkv_rk, rwkv_ln_w, rwkv_ln_b, mlstm_conv_w, mlstm_conv_b, mlstm_b_i, mlstm_b_f, mlstm_norm_g, w_out, norm2_g, router_w, router_b, moe_w_gu, moe_b_gu, moe_w_dn, moe_b_dn, final_g):
    B, T, D = x.shape
    N = B * T
    x2 = x.reshape(N, D)
    l = 0

    mod = _adaln(c, ada_w[l], ada_b[l])
    sh_m, sc_m, gt_m, sh_f, sc_f, gt_f = [m.reshape(B, 1, D) for m in jnp.split(mod, 6, axis=-1)]

    proj = _in_proj(x2, norm1_g[l], sc_m, sh_m, _relayout_w_in(w_in[l]), T)

    mu = rwkv_mu[l]
    mu_r, mu_wl, mu_k, mu_v, mu_al, mu_gl = (mu[0:1024], mu[1024:1088], mu[1088:2112], mu[2112:3136],
                                             mu[3136:3200], mu[3200:3360])
    row = lambda a: a.reshape(1, -1)
    zrows = lambda n: jnp.zeros((n, RWKV_DIM), F32)
    rw = {
        "mu_r": row(mu_r), "mu_k": row(mu_k), "mu_v": row(mu_v),
        "mu_gl": row(jnp.concatenate([mu_gl, jnp.zeros((96,), F32)])),
        "mu_lo": row(jnp.concatenate([mu_wl, mu_al])),
        "w0": row(rwkv_w0[l]), "a0": row(rwkv_a0[l]), "kk": row(rwkv_kk[l]), "ka": row(rwkv_ka[l]),
        "rk": row(rwkv_rk[l]), "ln_w": row(rwkv_ln_w[l]), "ln_b": row(rwkv_ln_b[l]),
        "w2": jnp.concatenate([rwkv_w2[l], zrows(A_LORA)], axis=0),
        "a2": jnp.concatenate([zrows(DECAY_LORA), rwkv_a2[l]], axis=0),
        "g2": jnp.concatenate([rwkv_g2[l], zrows(256 - G_LORA)], axis=0).astype(BF16),
    }
    y_rwkv = _rwkv(proj, B, T, rw)

    cw = mlstm_conv_w[l]
    cb = mlstm_conv_b[l]
    ml = {
        "cw_q": cw[:, :MLSTM_DIM], "cw_k": cw[:, MLSTM_DIM:],
        "cb_q": row(cb[:MLSTM_DIM]), "cb_k": row(cb[MLSTM_DIM:]),
        "gbias": row(jnp.concatenate([mlstm_b_i[l], mlstm_b_f[l], jnp.zeros((120,), F32)])),
        "norm_g": row(mlstm_norm_g[l]),
    }
    y_mlstm = _mlstm(proj, B, T, ml)

    rw_p = jnp.concatenate([router_w[l], jnp.zeros((D, 128 - N_EXPERTS), F32)], axis=1)
    rb_p = jnp.concatenate([router_b[l], jnp.zeros((128 - N_EXPERTS,), F32)]).reshape(1, 128)
    x1, h2, idx, gate_cols, rank, cnt = _out_route(
        y_rwkv, y_mlstm, x2, w_out[l].astype(BF16), gt_m, norm2_g[l], sc_f, sh_f, rw_p, rb_p, T)

    dest = _dest(idx, rank, cnt).reshape(-1)

    counts = cnt[:, 0]
    per = GROUP_ROWS // MOE_BLOCK
    nblk = (counts + MOE_BLOCK - 1) // MOE_BLOCK
    blk_start = jnp.cumsum(nblk) - nblk
    ngrp = (nblk + per - 1) // per
    gend = jnp.cumsum(ngrp)
    gstart = gend - ngrp
    n_groups = N_EXPERTS + (N * TOP_K) // GROUP_ROWS
    gid = jnp.arange(n_groups, dtype=jnp.int32)
    ge_raw = jnp.minimum(jnp.searchsorted(gend, gid, side="right"), N_EXPERTS - 1).astype(jnp.int32)
    valid = gid < gend[-1]
    last_e = jnp.minimum(jnp.searchsorted(gend, jnp.maximum(gend[-1] - 1, 0), side="right"),
                         N_EXPERTS - 1).astype(jnp.int32)
    ge = jnp.where(valid, ge_raw, last_e).astype(jnp.int32)
    within = gid - gstart[ge_raw]
    gs = jnp.where(valid, blk_start[ge_raw] + within * per, 0).astype(jnp.int32)
    gn = jnp.where(valid, jnp.minimum(nblk[ge_raw] - within * per, per), 0).astype(jnp.int32)

    n_rows = ((N * TOP_K) // MOE_BLOCK + N_EXPERTS) * MOE_BLOCK
    xs = _dispatch(dest, h2, n_rows)
    ys = _experts(ge, gs, gn, xs, moe_w_gu[l], moe_b_gu[l], moe_w_dn[l], moe_b_dn[l])
    out = _combine(dest, ys, x1, gate_cols, gt_f, final_g, T)
    return out.reshape(B, T, D)
```

```python
import functools

import jax
import jax.numpy as jnp
from jax import lax
from jax.experimental import pallas as pl
from jax.experimental.pallas import tpu as pltpu

F32 = jnp.float32
BF16 = jnp.bfloat16
HIGHEST = lax.Precision.HIGHEST

D_MODEL = 2048
CHUNK = 64
RMS_EPS = 1e-5
RWKV_DIM = 1024
RWKV_HEAD = 64
DECAY_LORA = 64
A_LORA = 64
G_LORA = 160
GN_EPS = 64e-5
MLSTM_DIM = 1024
MLSTM_HEADS = 4
MLSTM_HEAD = 256
CONV_W = 4
HEAD_NORM_EPS = 1e-6
N_EXPERTS = 32
TOP_K = 4
D_FF = 2048
SWIGLU_LIMIT = 7.0
SWIGLU_ALPHA = 1.702
MOE_BLOCK = 256

COL_R, COL_K, COL_V, COL_Q, COL_MK, COL_MV, COL_MO = (i * 1024 for i in range(7))
COL_GL = 7168
COL_LORA = 7424
COL_IF = 7552
IN_COLS_P = 7680
QUAD = 256

VMEM_LIMIT = 56 * 1024 * 1024


def _cparams(sem, vmem=VMEM_LIMIT):
    return pltpu.CompilerParams(dimension_semantics=sem, vmem_limit_bytes=vmem)


def _dot(a, b, **kw):
    return jnp.dot(a, b, preferred_element_type=F32, **kw)


def _dot_nt(a, b, **kw):
    return lax.dot_general(a, b, (((1,), (1,)), ((), ())), preferred_element_type=F32, **kw)


def _dot_tn(a, b, **kw):
    return lax.dot_general(a, b, (((0,), (0,)), ((), ())), preferred_element_type=F32, **kw)


def _split3(x):
    h = x.astype(BF16)
    r1 = x - h.astype(F32)
    m = r1.astype(BF16)
    l = (r1 - m.astype(F32)).astype(BF16)
    return h, m, l


def _iota(shape, dim):
    return lax.broadcasted_iota(jnp.int32, shape, dim)


def _adaln_kernel(c_ref, w_ref, b_ref, o_ref):
    c = c_ref[...]
    s = c * jax.nn.sigmoid(c)
    o_ref[...] = _dot(s, w_ref[...], precision=HIGHEST) + b_ref[...]


def _adaln(c, ada_w, ada_b):
    B, D = c.shape
    n_out = ada_w.shape[1]
    tn = 1024
    c8 = jnp.zeros((8, D), F32).at[:B].set(c)
    out = pl.pallas_call(
        _adaln_kernel,
        grid=(n_out // tn,),
        in_specs=[pl.BlockSpec((8, D), lambda j: (0, 0)),
                  pl.BlockSpec((D, tn), lambda j: (0, j)),
                  pl.BlockSpec((1, tn), lambda j: (0, j))],
        out_specs=pl.BlockSpec((8, tn), lambda j: (0, j)),
        out_shape=jax.ShapeDtypeStruct((8, n_out), F32),
        compiler_params=_cparams(("parallel",)),
        name="adaln",
    )(c8, ada_w, ada_b.reshape(1, n_out))
    return out[:B]


def _inproj_kernel(x_ref, g_ref, sc_ref, sh_ref, w_ref, o_ref, h_scr):
    @pl.when(pl.program_id(1) == 0)
    def _():
        x = x_ref[...]
        y = x * lax.rsqrt(jnp.mean(x * x, axis=-1, keepdims=True) + RMS_EPS)
        h = (y * g_ref[...]) * (1.0 + sc_ref[0]) + sh_ref[0]
        h_scr[...] = h.astype(BF16)

    o_ref[...] = _dot(h_scr[...], w_ref[...])


def _in_proj(x2, norm_g, sc, sh, w_p, seq):
    N, D = x2.shape
    tm, tn = 1024, 768
    tiles_per_b = seq // tm
    ncols = w_p.shape[1]
    return pl.pallas_call(
        _inproj_kernel,
        grid=(N // tm, ncols // tn),
        in_specs=[pl.BlockSpec((tm, D), lambda i, j: (i, 0)),
                  pl.BlockSpec((1, D), lambda i, j: (0, 0)),
                  pl.BlockSpec((1, 1, D), lambda i, j: (i // tiles_per_b, 0, 0)),
                  pl.BlockSpec((1, 1, D), lambda i, j: (i // tiles_per_b, 0, 0)),
                  pl.BlockSpec((D, tn), lambda i, j: (0, j))],
        out_specs=pl.BlockSpec((tm, tn), lambda i, j: (i, j)),
        out_shape=jax.ShapeDtypeStruct((N, ncols), F32),
        scratch_shapes=[pltpu.VMEM((tm, D), BF16)],
        compiler_params=_cparams(("parallel", "arbitrary")),
        name="in_proj",
    )(x2, norm_g.reshape(1, D), sc, sh, w_p)


def _shift_rows(x, prev_row):
    rolled = pltpu.roll(x, 1, axis=0)
    row = _iota(x.shape, 0)
    return jnp.where(row == 0, prev_row, rolled)


def _rwkv_kernel(r_ref, k_ref, v_ref, gl_ref, lo_ref,
                 mu_r_ref, mu_k_ref, mu_v_ref, mu_gl_ref, mu_lo_ref,
                 w0_ref, a0_ref, kkp_ref, kap_ref, rk_ref, lnw_ref, lnb_ref,
                 w2_ref, a2_ref, g2_ref,
                 o_ref,
                 s_scr, prev_scr):
    L = CHUNK
    c = pl.program_id(1)

    @pl.when(c == 0)
    def _():
        s_scr[...] = jnp.zeros_like(s_scr)
        prev_scr[...] = jnp.zeros_like(prev_scr)

    def lerp(ref, mu_ref, lo, width):
        x = ref[...]
        prev = prev_scr[0:1, lo:lo + width]
        out = x + (_shift_rows(x, prev) - x) * mu_ref[...]
        prev_scr[0:1, lo:lo + width] = x[L - 1:L, :]
        return out

    r = lerp(r_ref, mu_r_ref, 0, RWKV_DIM)
    k = lerp(k_ref, mu_k_ref, 1024, RWKV_DIM)
    v = lerp(v_ref, mu_v_ref, 2048, RWKV_DIM)
    gl = lerp(gl_ref, mu_gl_ref, 3072, 256)
    lo = lerp(lo_ref, mu_lo_ref, 3328, 128)

    w = -jax.nn.softplus(-(w0_ref[...] + _dot(jnp.tanh(lo), w2_ref[...], precision=HIGHEST))) - 0.5
    lw = -jnp.exp(w)
    a = jax.nn.sigmoid(a0_ref[...] + _dot(lo, a2_ref[...], precision=HIGHEST))
    g = _dot(jax.nn.sigmoid(gl).astype(BF16), g2_ref[...])
    kkraw = k * kkp_ref[...]
    k2 = k * (1.0 + (a - 1.0) * kap_ref[...])

    ltri = jnp.where(_iota((L, L), 1) <= _iota((L, L), 0), 1.0, 0.0).astype(BF16)
    qr = _iota((QUAD, QUAD), 0)
    qc = _iota((QUAD, QUAD), 1)
    bd_mask = (qr // RWKV_HEAD) == (qc // RWKV_HEAD)
    bd = jnp.where(bd_mask, 1.0, 0.0).astype(BF16)
    strict_bd = bd_mask & ((qc % L) < (qr % L))
    incl_bd = bd_mask & ((qc % L) <= (qr % L))
    lane_head = _iota((1, QUAD), 1) // RWKV_HEAD
    n_heads = QUAD // RWKV_HEAD
    quads = range(RWKV_DIM // QUAD)

    def head_sum(x):
        h, m, l = _split3(x)
        return _dot(h, bd) + _dot(m, bd) + _dot(l, bd)

    def time_cumsum(x):
        h, m, l = _split3(x)
        return _dot(ltri, h) + _dot(ltri, m) + _dot(ltri, l)

    def stack_heads(x):
        return jnp.concatenate([jnp.where(lane_head == h, x, jnp.zeros_like(x)) for h in range(n_heads)], axis=0)

    def unstack_heads(x):
        out = x[0:L]
        for h in range(1, n_heads):
            out = out + x[h * L:(h + 1) * L]
        return out

    sls = [slice(q * QUAD, (q + 1) * QUAD) for q in quads]

    kk = []
    for sl in sls:
        kkr = kkraw[:, sl]
        kk.append(kkr / jnp.maximum(jnp.sqrt(head_sum(kkr * kkr)), 1e-12))
    cl = [time_cumsum(lw[:, sl]) for sl in sls]

    at, rt, bt, kt, p_last = [], [], [], [], []
    for q, sl in enumerate(sls):
        p_t = jnp.exp(cl[q])
        p_prev = jnp.exp(cl[q] - lw[:, sl])
        p_inv = jnp.exp(-cl[q])
        at.append((-kk[q] * p_prev).astype(BF16))
        rt.append((r[:, sl] * p_t).astype(BF16))
        bt.append((kk[q] * a[:, sl] * p_inv).astype(BF16))
        kt.append((k2[:, sl] * p_inv).astype(BF16))
        p_last.append(p_t[L - 1:L, :])

    s0 = [s_scr[q] for q in quads]
    n_pow, a_ak, a_rb, a_rk, rb4, rk4, vst, xs_st, y_state = [], [], [], [], [], [], [], [], []
    for q, sl in enumerate(sls):
        lst = jnp.concatenate([stack_heads(at[q]), stack_heads(rt[q])], axis=0)
        rb4.append(jnp.concatenate([bt[q]] * n_heads, axis=0))
        rk4.append(jnp.concatenate([kt[q]] * n_heads, axis=0))
        gb = _dot_nt(lst, rb4[q])
        gk = _dot_nt(lst, rk4[q])
        n_pow.append(jnp.where(strict_bd, gb[:QUAD], 0.0))
        a_rb.append(jnp.where(incl_bd, gb[QUAD:], 0.0).astype(BF16))
        a_ak.append(jnp.where(strict_bd, gk[:QUAD], 0.0).astype(BF16))
        a_rk.append(jnp.where(incl_bd, gk[QUAD:], 0.0).astype(BF16))
        vst.append(stack_heads(v[:, sl].astype(BF16)))
        xy = _dot_nt(jnp.concatenate([at[q], rt[q]], axis=0), s0[q].astype(BF16))
        xs_st.append(stack_heads(xy[:L]))
        y_state.append(xy[L:])

    u = [xs_st[q] + _dot(a_ak[q], vst[q]) for q in quads]
    for j in range(6):
        nb = [n_pow[q].astype(BF16) for q in quads]
        u = [u[q] + _dot(nb[q], u[q].astype(BF16)) for q in quads]
        if j < 5:
            n_pow = [_dot(nb[q], nb[q]) for q in quads]

    ub = [u[q].astype(BF16) for q in quads]
    y_st = [_dot(a_rb[q], ub[q]) + _dot(a_rk[q], vst[q]) for q in quads]
    for q in quads:
        z = s0[q] + _dot_tn(ub[q], rb4[q]) + _dot_tn(vst[q], rk4[q])
        s_scr[q] = jnp.where(bd_mask, z, 0.0) * p_last[q]

    inv_n = 1.0 / RWKV_HEAD
    y = [y_state[q] + unstack_heads(y_st[q]) for q in quads]
    mean = [head_sum(y[q]) * inv_n for q in quads]
    d = [y[q] - mean[q] for q in quads]
    var = [head_sum(d[q] * d[q]) * inv_n for q in quads]
    bonus = [head_sum(r[:, sl] * k2[:, sl] * rk_ref[:, sl]) * v[:, sl] for sl in sls]
    for q, sl in enumerate(sls):
        yn = d[q] * lax.rsqrt(var[q] + GN_EPS) * lnw_ref[:, sl] + lnb_ref[:, sl]
        o_ref[:, sl] = ((yn + bonus[q]) * g[:, sl]).astype(o_ref.dtype)


def _rwkv(proj, B, T, pr):
    NC = T // CHUNK
    L = CHUNK
    row = lambda b, c: b * NC + c

    def pspec(width, colblk):
        return pl.BlockSpec((L, width), lambda b, c: (row(b, c), colblk))

    def vec(width):
        return pl.BlockSpec((1, width), lambda b, c: (0, 0))

    def mat(rows):
        return pl.BlockSpec((rows, RWKV_DIM), lambda b, c: (0, 0))

    return pl.pallas_call(
        _rwkv_kernel,
        grid=(B, NC),
        in_specs=[pspec(1024, COL_R // 1024), pspec(1024, COL_K // 1024), pspec(1024, COL_V // 1024),
                  pspec(256, COL_GL // 256), pspec(128, COL_LORA // 128),
                  vec(1024), vec(1024), vec(1024), vec(256), vec(128),
                  vec(1024), vec(1024), vec(1024), vec(1024), vec(1024), vec(1024), vec(1024),
                  mat(128), mat(128), mat(256)],
        out_specs=pl.BlockSpec((L, RWKV_DIM), lambda b, c: (row(b, c), 0)),
        out_shape=jax.ShapeDtypeStruct((B * T, RWKV_DIM), BF16),
        scratch_shapes=[pltpu.VMEM((RWKV_DIM // QUAD, QUAD, QUAD), F32),
                        pltpu.VMEM((8, 3072 + 256 + 128), F32)],
        compiler_params=_cparams(("parallel", "arbitrary")),
        name="rwkv",
    )(proj, proj, proj, proj, proj,
      pr["mu_r"], pr["mu_k"], pr["mu_v"], pr["mu_gl"], pr["mu_lo"],
      pr["w0"], pr["a0"], pr["kk"], pr["ka"], pr["rk"], pr["ln_w"], pr["ln_b"],
      pr["w2"], pr["a2"], pr["g2"])


def _mlstm_kernel(q_ref, k_ref, v_ref, o_ref, gt_ref,
                  cwq_ref, cwk_ref, cbq_ref, cbk_ref, gbias_ref, ng_ref,
                  out_ref,
                  c_scr, n_scr, m_scr, prevq_scr, prevk_scr):
    L = CHUNK
    c = pl.program_id(1)

    @pl.when(c == 0)
    def _():
        c_scr[...] = jnp.zeros_like(c_scr)
        n_scr[...] = jnp.zeros_like(n_scr)
        m_scr[...] = jnp.zeros_like(m_scr)
        prevq_scr[...] = jnp.zeros_like(prevq_scr)
        prevk_scr[...] = jnp.zeros_like(prevk_scr)

    def conv_silu(ref, prev_scr, cw_ref, cb_ref):
        x = ref[...]
        prev8 = prev_scr[...]
        row8 = _iota(prev8.shape, 0)
        out = cb_ref[...] + x * cw_ref[CONV_W - 1:CONV_W, :]
        for d in range(1, CONV_W):
            rolled = pltpu.roll(x, d, axis=0)
            first8 = jnp.where(row8 < d, pltpu.roll(prev8, d, axis=0), rolled[:8])
            xs = jnp.concatenate([first8, rolled[8:]], axis=0)
            out = out + xs * cw_ref[CONV_W - 1 - d:CONV_W - d, :]
        prev_scr[...] = x[L - 8:L, :]
        return out * jax.nn.sigmoid(out)

    qf = conv_silu(q_ref, prevq_scr, cwq_ref, cbq_ref)
    kf = conv_silu(k_ref, prevk_scr, cwk_ref, cbk_ref) * (MLSTM_HEAD ** -0.5)
    vf = v_ref[...]

    z = gt_ref[...] + gbias_ref[...]
    lane = _iota(z.shape, 1)
    gc = jnp.where(lane < MLSTM_HEADS, z,
                   jnp.where(lane < 2 * MLSTM_HEADS, jax.nn.log_sigmoid(z), 0.0))
    row = _iota((L, L), 0)
    col = _iota((L, L), 1)
    causal = col <= row
    ltri = jnp.where(causal, 1.0, 0.0).astype(BF16)
    gh, gm, glo = _split3(gc)
    cum = _dot(ltri, gh) + _dot(ltri, gm) + _dot(ltri, glo)
    gc_t = gc.T
    cum_t = cum.T

    for h in range(MLSTM_HEADS):
        sl = slice(h * MLSTM_HEAD, (h + 1) * MLSTM_HEAD)
        qh, kh, vh = qf[:, sl], kf[:, sl], vf[:, sl]
        li_row = gc_t[h:h + 1, :]
        li_col = gc[:, h:h + 1]
        b_row = cum_t[MLSTM_HEADS + h:MLSTM_HEADS + h + 1, :]
        b_col = cum[:, MLSTM_HEADS + h:MLSTM_HEADS + h + 1]
        m_prev = m_scr[h:h + 1, 0:1]
        c_prev = c_scr[h]
        n_prev = n_scr[h:h + 1, :]

        a_inter = b_col + m_prev
        dm = jnp.where(causal, b_col - b_row + li_row, -jnp.inf)
        m_t = jnp.maximum(a_inter, jnp.max(dm, axis=-1, keepdims=True))
        w_inter = jnp.exp(a_inter - m_t)
        wmat = jnp.exp(dm - m_t)
        qb, kb, vb = qh.astype(BF16), kh.astype(BF16), vh.astype(BF16)
        s = _dot_nt(qb, kb) * wmat
        num = w_inter * _dot_nt(qb, c_prev.astype(BF16)) + _dot(s.astype(BF16), vb)
        den = w_inter * jnp.sum(qh * n_prev, axis=-1, keepdims=True) + jnp.sum(s, axis=-1, keepdims=True)
        hh = num / jnp.maximum(jnp.abs(den), jnp.exp(-m_t))

        m_new = m_t[L - 1:L, :]
        b_last = b_col[L - 1:L, :]
        g_state = jnp.exp(b_last + m_prev - m_new)
        w_s = jnp.exp(b_last - b_col + li_col - m_new)
        c_scr[h] = g_state * c_prev + _dot_tn((vh * w_s).astype(BF16), kb)
        n_scr[h:h + 1, :] = g_state * n_prev + jnp.sum(kh * w_s, axis=0, keepdims=True)
        m_scr[h:h + 1, :] = jnp.broadcast_to(m_new, (1, m_scr.shape[1]))

        hn = hh * lax.rsqrt(jnp.mean(hh * hh, axis=-1, keepdims=True) + HEAD_NORM_EPS)
        out_ref[:, sl] = ((hn * ng_ref[:, sl]) * jax.nn.sigmoid(o_ref[:, sl])).astype(out_ref.dtype)


def _mlstm(proj, B, T, pr):
    NC = T // CHUNK
    L = CHUNK
    row = lambda b, c: b * NC + c

    def pspec(width, colblk):
        return pl.BlockSpec((L, width), lambda b, c: (row(b, c), colblk))

    def cst(shape):
        return pl.BlockSpec(shape, lambda b, c: (0, 0))

    return pl.pallas_call(
        _mlstm_kernel,
        grid=(B, NC),
        in_specs=[pspec(1024, COL_Q // 1024), pspec(1024, COL_MK // 1024), pspec(1024, COL_MV // 1024),
                  pspec(1024, COL_MO // 1024), pspec(128, COL_IF // 128),
                  cst((CONV_W, 1024)), cst((CONV_W, 1024)), cst((1, 1024)), cst((1, 1024)),
                  cst((1, 128)), cst((1, 1024))],
        out_specs=pl.BlockSpec((L, MLSTM_DIM), lambda b, c: (row(b, c), 0)),
        out_shape=jax.ShapeDtypeStruct((B * T, MLSTM_DIM), BF16),
        scratch_shapes=[pltpu.VMEM((MLSTM_HEADS, MLSTM_HEAD, MLSTM_HEAD), F32),
                        pltpu.VMEM((8, MLSTM_HEAD), F32),
                        pltpu.VMEM((8, 128), F32),
                        pltpu.VMEM((8, 1024), F32),
                        pltpu.VMEM((8, 1024), F32)],
        compiler_params=_cparams(("parallel", "arbitrary")),
        name="mlstm",
    )(proj, proj, proj, proj, proj,
      pr["cw_q"], pr["cw_k"], pr["cb_q"], pr["cb_k"], pr["gbias"], pr["norm_g"])


def _out_route_kernel(yr_ref, ym_ref, x_ref, wo_ref, gt_ref, g2_ref, sc_ref, sh_ref, rw_ref, rb_ref,
                      x1_ref, h2_ref, idx_ref, gate_ref, rank_ref, cnt_ref,
                      carry_scr):
    i = pl.program_id(0)
    tm = x_ref.shape[0]

    @pl.when(i == 0)
    def _():
        carry_scr[...] = jnp.zeros_like(carry_scr)

    mix = _dot(yr_ref[...], wo_ref[0:RWKV_DIM, :]) + _dot(ym_ref[...], wo_ref[RWKV_DIM:, :])
    x1 = x_ref[...] + gt_ref[0] * mix
    x1_ref[...] = x1
    y = x1 * lax.rsqrt(jnp.mean(x1 * x1, axis=-1, keepdims=True) + RMS_EPS)
    h2 = (y * g2_ref[...]) * (1.0 + sc_ref[0]) + sh_ref[0]
    hbits = lax.bitcast_convert_type(h2.astype(BF16).astype(F32), jnp.uint32)
    half = h2.shape[1] // 2
    h2_ref[...] = (hbits[:, :half] >> 16) | (hbits[:, half:] & jnp.uint32(0xFFFF0000))

    logits = _dot(h2, rw_ref[...], precision=HIGHEST) + rb_ref[...]
    lt = logits.T[:N_EXPERTS, :]
    e_iota = _iota(lt.shape, 0)
    onehots, vals, idxs = [], [], []
    for _ in range(TOP_K):
        mx = jnp.max(lt, axis=0, keepdims=True)
        idx = jnp.min(jnp.where(lt == mx, e_iota, N_EXPERTS), axis=0, keepdims=True)
        sel = e_iota == idx
        onehots.append(sel)
        vals.append(mx)
        idxs.append(idx)
        lt = jnp.where(sel, -jnp.inf, lt)
    exps = [jnp.exp(vv - vals[0]) for vv in vals]
    denom = exps[0] + exps[1] + exps[2] + exps[3]
    gates = [e / denom for e in exps]

    member = jnp.zeros(onehots[0].shape, F32)
    for sel in onehots:
        member = member + jnp.where(sel, 1.0, 0.0)
    ur = _iota((tm, tm), 0)
    uc = _iota((tm, tm), 1)
    ustrict = jnp.where(ur < uc, 1.0, 0.0).astype(BF16)
    before = _dot(member.astype(BF16), ustrict) + carry_scr[:, 0:1]
    for j in range(TOP_K):
        rank = jnp.sum(jnp.where(onehots[j], before, 0.0), axis=0, keepdims=True)
        rank_ref[j:j + 1, :] = rank.astype(jnp.int32)
        idx_ref[j:j + 1, :] = idxs[j]
    new_carry = carry_scr[...] + jnp.sum(member, axis=1, keepdims=True)
    carry_scr[...] = new_carry
    cnt_ref[...] = new_carry.astype(jnp.int32)

    grows = jnp.concatenate(gates + [jnp.zeros((128 - TOP_K, tm), F32)], axis=0)
    gate_ref[...] = grows.T


def _out_route(yr, ym, x2, w_out_b, gt, g2, sc, sh, rw_p, rb_p, seq):
    N, D = x2.shape
    tm = 512
    tiles_per_b = seq // tm
    bvec = pl.BlockSpec((1, 1, D), lambda i: (i // tiles_per_b, 0, 0))
    return pl.pallas_call(
        _out_route_kernel,
        grid=(N // tm,),
        in_specs=[pl.BlockSpec((tm, RWKV_DIM), lambda i: (i, 0)),
                  pl.BlockSpec((tm, MLSTM_DIM), lambda i: (i, 0)),
                  pl.BlockSpec((tm, D), lambda i: (i, 0)),
                  pl.BlockSpec((D, D), lambda i: (0, 0)),
                  bvec,
                  pl.BlockSpec((1, D), lambda i: (0, 0)),
                  bvec, bvec,
                  pl.BlockSpec((D, 128), lambda i: (0, 0)),
                  pl.BlockSpec((1, 128), lambda i: (0, 0))],
        out_specs=[pl.BlockSpec((tm, D), lambda i: (i, 0)),
                   pl.BlockSpec((tm, D // 2), lambda i: (i, 0)),
                   pl.BlockSpec((TOP_K, tm), lambda i: (0, i)),
                   pl.BlockSpec((tm, 128), lambda i: (i, 0)),
                   pl.BlockSpec((TOP_K, tm), lambda i: (0, i)),
                   pl.BlockSpec((N_EXPERTS, 128), lambda i: (0, 0))],
        out_shape=[jax.ShapeDtypeStruct((N, D), F32),
                   jax.ShapeDtypeStruct((N, D // 2), jnp.uint32),
                   jax.ShapeDtypeStruct((TOP_K, N), jnp.int32),
                   jax.ShapeDtypeStruct((N, 128), F32),
                   jax.ShapeDtypeStruct((TOP_K, N), jnp.int32),
                   jax.ShapeDtypeStruct((N_EXPERTS, 128), jnp.int32)],
        scratch_shapes=[pltpu.VMEM((N_EXPERTS, 128), F32)],
        compiler_params=_cparams(("arbitrary",)),
        name="out_route",
    )(yr, ym, x2, w_out_b, gt, g2.reshape(1, D), sc, sh, rw_p, rb_p)


def _dest_kernel(idx_ref, rank_ref, cnt_ref, dest_ref):
    cnt = cnt_ref[:, 0:1].astype(F32)
    padded = jnp.floor((cnt + (MOE_BLOCK - 1)) * (1.0 / MOE_BLOCK)) * MOE_BLOCK
    er = _iota((N_EXPERTS, N_EXPERTS), 0)
    ec = _iota((N_EXPERTS, N_EXPERTS), 1)
    before = jnp.where(ec < er, 1.0, 0.0)
    pstart = _dot(before, jnp.broadcast_to(padded, (N_EXPERTS, 128)), precision=HIGHEST)[:, 0:1]
    e_iota = _iota((N_EXPERTS, idx_ref.shape[1]), 0)
    for j in range(TOP_K):
        sel = e_iota == idx_ref[j:j + 1, :]
        base = jnp.sum(jnp.where(sel, pstart, 0.0), axis=0, keepdims=True)
        dest_ref[j:j + 1, :] = base.astype(jnp.int32) + rank_ref[j:j + 1, :]


def _dest(idx, rank, cnt):
    N = idx.shape[1]
    tn = 2048
    return pl.pallas_call(
        _dest_kernel,
        grid=(N // tn,),
        in_specs=[pl.BlockSpec((TOP_K, tn), lambda i: (0, i)),
                  pl.BlockSpec((TOP_K, tn), lambda i: (0, i)),
                  pl.BlockSpec((N_EXPERTS, 128), lambda i: (0, 0))],
        out_specs=pl.BlockSpec((TOP_K, tn), lambda i: (0, i)),
        out_shape=jax.ShapeDtypeStruct((TOP_K, N), jnp.int32),
        compiler_params=_cparams(("parallel",)),
        name="dest",
    )(idx, rank, cnt)


def _dispatch_kernel(dest_ref, h_ref, xs_in_ref, xs_ref, sem):
    del xs_in_ref
    i = pl.program_id(0)
    tm = h_ref.shape[0]
    n_tok = pl.num_programs(0) * tm

    def copy(t, j):
        d = dest_ref[j * n_tok + i * tm + t]
        return pltpu.make_async_copy(h_ref.at[pl.ds(t, 1)], xs_ref.at[pl.ds(d, 1)], sem)

    def issue(t, carry):
        for j in range(TOP_K):
            copy(t, j).start()
        return carry

    lax.fori_loop(0, tm, issue, 0)

    def drain(t, carry):
        for j in range(TOP_K):
            copy(t, j).wait()
        return carry

    lax.fori_loop(0, tm, drain, 0)


def _dispatch(dest_flat, h2, n_rows):
    N, D = h2.shape
    tm = 256
    xs0 = jnp.zeros((n_rows, D), h2.dtype)
    return pl.pallas_call(
        _dispatch_kernel,
        grid_spec=pltpu.PrefetchScalarGridSpec(
            num_scalar_prefetch=1,
            grid=(N // tm,),
            in_specs=[pl.BlockSpec((tm, D), lambda i, d: (i, 0)),
                      pl.BlockSpec(memory_space=pl.ANY)],
            out_specs=pl.BlockSpec(memory_space=pl.ANY),
            scratch_shapes=[pltpu.SemaphoreType.DMA(())]),
        out_shape=jax.ShapeDtypeStruct((n_rows, D), h2.dtype),
        input_output_aliases={2: 0},
        compiler_params=_cparams(("arbitrary",)),
        name="dispatch",
    )(dest_flat, h2, xs0)


GROUP_ROWS = 2048
FF_TILE = 256


def _experts_kernel(ge_ref, gs_ref, gn_ref,
                    xs_ref, wg_ref, wu_ref, wd_ref, bg_ref, bu_ref, bd_ref,
                    ys_ref,
                    xbuf, xb16, acc, wgb, wub, wdb, sem):
    g = pl.program_id(0)
    j = pl.program_id(1)
    nblk = gn_ref[g]
    row0 = pl.multiple_of(gs_ref[g] * MOE_BLOCK, MOE_BLOCK)

    def in_copy(s):
        off = pl.multiple_of(s * MOE_BLOCK, MOE_BLOCK)
        return pltpu.make_async_copy(xs_ref.at[pl.ds(row0 + off, MOE_BLOCK)],
                                     xbuf.at[pl.ds(off, MOE_BLOCK)], sem.at[0])

    def out_copy(s):
        off = pl.multiple_of(s * MOE_BLOCK, MOE_BLOCK)
        return pltpu.make_async_copy(acc.at[pl.ds(off, MOE_BLOCK)],
                                     ys_ref.at[pl.ds(row0 + off, MOE_BLOCK)], sem.at[1])

    def for_blocks(fn):
        def body(s, carry):
            fn(s)
            return carry
        lax.fori_loop(0, nblk, body, 0)

    @pl.when(nblk > 0)
    def _():
        @pl.when(j == 0)
        def _():
            for_blocks(lambda s: in_copy(s).start())
            for_blocks(lambda s: in_copy(s).wait())

        wgb[...] = wg_ref[0].astype(BF16)
        wub[...] = wu_ref[0].astype(BF16)
        wdb[...] = wd_ref[0].astype(BF16)

        for n in range(1, GROUP_ROWS // MOE_BLOCK + 1):
            m = n * MOE_BLOCK

            @pl.when(nblk == n)
            def _(m=m):
                @pl.when(j == 0)
                def _():
                    w = xbuf[0:m, :]
                    half = w.shape[1]
                    xb16[0:m, 0:half] = lax.bitcast_convert_type(w << 16, F32).astype(BF16)
                    xb16[0:m, half:] = lax.bitcast_convert_type(w & jnp.uint32(0xFFFF0000), F32).astype(BF16)
                    acc[0:m, :] = jnp.broadcast_to(bd_ref[0], (m, acc.shape[1]))

                xb = xb16[0:m, :]
                gate = _dot(xb, wgb[...]) + bg_ref[0]
                up = _dot(xb, wub[...]) + bu_ref[0]
                gate = jnp.minimum(gate, SWIGLU_LIMIT)
                up = jnp.clip(up, -SWIGLU_LIMIT, SWIGLU_LIMIT)
                act = (up + 1.0) * (gate * jax.nn.sigmoid(SWIGLU_ALPHA * gate))
                acc[0:m, :] = acc[0:m, :] + _dot(act.astype(BF16), wdb[...])

        @pl.when(j == pl.num_programs(1) - 1)
        def _():
            for_blocks(lambda s: out_copy(s).start())
            for_blocks(lambda s: out_copy(s).wait())

    @pl.when((g == pl.num_programs(0) - 1) & (j == pl.num_programs(1) - 1))
    def _():
        acc[0:MOE_BLOCK, :] = jnp.zeros((MOE_BLOCK, acc.shape[1]), acc.dtype)

        def tail_copy(s):
            off = pl.multiple_of(s * MOE_BLOCK, MOE_BLOCK)
            return pltpu.make_async_copy(acc.at[pl.ds(0, MOE_BLOCK)], ys_ref.at[pl.ds(off, MOE_BLOCK)], sem.at[1])

        def over_tail(fn):
            def body(s, carry):
                fn(s)
                return carry
            lax.fori_loop(gn_ref[pl.num_programs(0)], ys_ref.shape[0] // MOE_BLOCK, body, 0)

        over_tail(lambda s: tail_copy(s).start())
        over_tail(lambda s: tail_copy(s).wait())


def _experts(ge, gs, gn, xs, w_gu, b_gu, w_dn, b_dn):
    R = xs.shape[0]
    D = D_MODEL
    E = w_gu.shape[0]
    G = ge.shape[0]
    J = D_FF // FF_TILE

    def jj(g, j, gn_ref):
        return jnp.where(gn_ref[g] > 0, j, J - 1)

    return pl.pallas_call(
        _experts_kernel,
        grid_spec=pltpu.PrefetchScalarGridSpec(
            num_scalar_prefetch=3,
            grid=(G, J),
            in_specs=[pl.BlockSpec(memory_space=pl.ANY),
                      pl.BlockSpec((1, D, FF_TILE), lambda g, j, ge, gs, gn: (ge[g], 0, jj(g, j, gn))),
                      pl.BlockSpec((1, D, FF_TILE), lambda g, j, ge, gs, gn: (ge[g], 0, J + jj(g, j, gn))),
                      pl.BlockSpec((1, FF_TILE, D), lambda g, j, ge, gs, gn: (ge[g], jj(g, j, gn), 0)),
                      pl.BlockSpec((1, 1, FF_TILE), lambda g, j, ge, gs, gn: (ge[g], 0, jj(g, j, gn))),
                      pl.BlockSpec((1, 1, FF_TILE), lambda g, j, ge, gs, gn: (ge[g], 0, J + jj(g, j, gn))),
                      pl.BlockSpec((1, 1, D), lambda g, j, ge, gs, gn: (ge[g], 0, 0))],
            out_specs=pl.BlockSpec(memory_space=pl.ANY),
            scratch_shapes=[pltpu.VMEM((GROUP_ROWS, D // 2), jnp.uint32),
                            pltpu.VMEM((GROUP_ROWS, D), BF16),
                            pltpu.VMEM((GROUP_ROWS, D), F32),
                            pltpu.VMEM((D, FF_TILE), BF16),
                            pltpu.VMEM((D, FF_TILE), BF16),
                            pltpu.VMEM((FF_TILE, D), BF16),
                            pltpu.SemaphoreType.DMA((2,))]),
        out_shape=jax.ShapeDtypeStruct((R, D), F32),
        compiler_params=_cparams(("arbitrary", "arbitrary")),
        name="experts",
    )(ge, gs, gn, xs, w_gu, w_gu, w_dn, b_gu.reshape(E, 1, 2 * D_FF), b_gu.reshape(E, 1, 2 * D_FF),
      b_dn.reshape(E, 1, D))


def _combine_kernel(dest_ref, ys_ref, x1_ref, gate_ref, gt_ref, fg_ref, o_ref, ybuf, sem):
    i = pl.program_id(0)
    tm = x1_ref.shape[0]
    n_tok = pl.num_programs(0) * tm

    def copy(t, j):
        d = dest_ref[j * n_tok + i * tm + t]
        return pltpu.make_async_copy(ys_ref.at[pl.ds(d, 1)], ybuf.at[j, pl.ds(t, 1)], sem)

    def issue(t, carry):
        for j in range(TOP_K):
            copy(t, j).start()
        return carry

    lax.fori_loop(0, tm, issue, 0)

    def drain(t, carry):
        for j in range(TOP_K):
            copy(t, j).wait()
        return carry

    lax.fori_loop(0, tm, drain, 0)

    gts = gate_ref[...]
    y = ybuf[0] * gts[:, 0:1]
    for j in range(1, TOP_K):
        y = y + ybuf[j] * gts[:, j:j + 1]
    x2 = x1_ref[...] + gt_ref[0] * y
    o_ref[...] = (x2 * lax.rsqrt(jnp.mean(x2 * x2, axis=-1, keepdims=True) + RMS_EPS)) * fg_ref[...]


def _combine(dest_flat, ys, x1, gate_cols, gt, final_g, seq):
    N, D = x1.shape
    tm = 256
    tiles_per_b = seq // tm
    return pl.pallas_call(
        _combine_kernel,
        grid_spec=pltpu.PrefetchScalarGridSpec(
            num_scalar_prefetch=1,
            grid=(N // tm,),
            in_specs=[pl.BlockSpec(memory_space=pl.ANY),
                      pl.BlockSpec((tm, D), lambda i, d: (i, 0)),
                      pl.BlockSpec((tm, 128), lambda i, d: (i, 0)),
                      pl.BlockSpec((1, 1, D), lambda i, d: (i // tiles_per_b, 0, 0)),
                      pl.BlockSpec((1, D), lambda i, d: (0, 0))],
            out_specs=pl.BlockSpec((tm, D), lambda i, d: (i, 0)),
            scratch_shapes=[pltpu.VMEM((TOP_K, tm, D), F32),
                            pltpu.SemaphoreType.DMA(())]),
        out_shape=jax.ShapeDtypeStruct((N, D), F32),
        compiler_params=_cparams(("arbitrary",)),
        name="combine",
    )(dest_flat, ys, x1, gate_cols, gt, final_g.reshape(1, D))


def _relayout_w_in(w_in):
    D = w_in.shape[0]
    o = 0
    seg = {}
    for name, n in (("r", 1024), ("wl", 64), ("k", 1024), ("v", 1024), ("al", 64), ("gl", 160),
                    ("q", 1024), ("mk", 1024), ("mv", 1024), ("mo", 1024), ("i", 4), ("f", 4)):
        seg[name] = w_in[:, o:o + n]
        o += n
    z = lambda n: jnp.zeros((D, n), w_in.dtype)
    return jnp.concatenate(
        [seg["r"], seg["k"], seg["v"], seg["q"], seg["mk"], seg["mv"], seg["mo"],
         seg["gl"], z(96), seg["wl"], seg["al"], seg["i"], seg["f"], z(120)], axis=1).astype(BF16)


def kernel(x, c, ada_w, ada_b, norm1_g, w_in, rwkv_mu, rwkv_w0, rwkv_w2, rwkv_a0, rwkv_a2, rwkv_g2, rwkv_kk, rwkv_ka, rwkv_rk, rwkv_ln_w, rwkv_ln_b, mlstm_conv_w, mlstm_conv_b, mlstm_b_i, mlstm_b_f, mlstm_norm_g, w_out, norm2_g, router_w, router_b, moe_w_gu, moe_b_gu, moe_w_dn, moe_b_dn, final_g):
    B, T, D = x.shape
    N = B * T
    x2 = x.reshape(N, D)
    l = 0

    mod = _adaln(c, ada_w[l], ada_b[l])
    sh_m, sc_m, gt_m, sh_f, sc_f, gt_f = [m.reshape(B, 1, D) for m in jnp.split(mod, 6, axis=-1)]

    proj = _in_proj(x2, norm1_g[l], sc_m, sh_m, _relayout_w_in(w_in[l]), T)

    mu = rwkv_mu[l]
    mu_r, mu_wl, mu_k, mu_v, mu_al, mu_gl = (mu[0:1024], mu[1024:1088], mu[1088:2112], mu[2112:3136],
                                             mu[3136:3200], mu[3200:3360])
    row = lambda a: a.reshape(1, -1)
    zrows = lambda n: jnp.zeros((n, RWKV_DIM), F32)
    rw = {
        "mu_r": row(mu_r), "mu_k": row(mu_k), "mu_v": row(mu_v),
        "mu_gl": row(jnp.concatenate([mu_gl, jnp.zeros((96,), F32)])),
        "mu_lo": row(jnp.concatenate([mu_wl, mu_al])),
        "w0": row(rwkv_w0[l]), "a0": row(rwkv_a0[l]), "kk": row(rwkv_kk[l]), "ka": row(rwkv_ka[l]),
        "rk": row(rwkv_rk[l]), "ln_w": row(rwkv_ln_w[l]), "ln_b": row(rwkv_ln_b[l]),
        "w2": jnp.concatenate([rwkv_w2[l], zrows(A_LORA)], axis=0),
        "a2": jnp.concatenate([zrows(DECAY_LORA), rwkv_a2[l]], axis=0),
        "g2": jnp.concatenate([rwkv_g2[l], zrows(256 - G_LORA)], axis=0).astype(BF16),
    }
    y_rwkv = _rwkv(proj, B, T, rw)

    cw = mlstm_conv_w[l]
    cb = mlstm_conv_b[l]
    ml = {
        "cw_q": cw[:, :MLSTM_DIM], "cw_k": cw[:, MLSTM_DIM:],
        "cb_q": row(cb[:MLSTM_DIM]), "cb_k": row(cb[MLSTM_DIM:]),
        "gbias": row(jnp.concatenate([mlstm_b_i[l], mlstm_b_f[l], jnp.zeros((120,), F32)])),
        "norm_g": row(mlstm_norm_g[l]),
    }
    y_mlstm = _mlstm(proj, B, T, ml)

    rw_p = jnp.concatenate([router_w[l], jnp.zeros((D, 128 - N_EXPERTS), F32)], axis=1)
    rb_p = jnp.concatenate([router_b[l], jnp.zeros((128 - N_EXPERTS,), F32)]).reshape(1, 128)
    x1, h2, idx, gate_cols, rank, cnt = _out_route(
        y_rwkv, y_mlstm, x2, w_out[l].astype(BF16), gt_m, norm2_g[l], sc_f, sh_f, rw_p, rb_p, T)

    dest = _dest(idx, rank, cnt).reshape(-1)

    counts = cnt[:, 0]
    per = GROUP_ROWS // MOE_BLOCK
    nblk = (counts + MOE_BLOCK - 1) // MOE_BLOCK
    blk_start = jnp.cumsum(nblk) - nblk
    ngrp = (nblk + per - 1) // per
    gend = jnp.cumsum(ngrp)
    gstart = gend - ngrp
    n_groups = N_EXPERTS + (N * TOP_K) // GROUP_ROWS
    gid = jnp.arange(n_groups, dtype=jnp.int32)
    ge_raw = jnp.minimum(jnp.searchsorted(gend, gid, side="right"), N_EXPERTS - 1).astype(jnp.int32)
    valid = gid < gend[-1]
    last_e = jnp.minimum(jnp.searchsorted(gend, jnp.maximum(gend[-1] - 1, 0), side="right"),
                         N_EXPERTS - 1).astype(jnp.int32)
    ge = jnp.where(valid, ge_raw, last_e).astype(jnp.int32)
    within = gid - gstart[ge_raw]
    gs = jnp.where(valid, blk_start[ge_raw] + within * per, 0).astype(jnp.int32)
    gn = jnp.where(valid, jnp.minimum(nblk[ge_raw] - within * per, per), 0).astype(jnp.int32)
    gn = jnp.concatenate([gn, jnp.sum(nblk, keepdims=True).astype(jnp.int32)])

    n_rows = ((N * TOP_K) // MOE_BLOCK + N_EXPERTS) * MOE_BLOCK
    xs = _dispatch(dest, h2, n_rows)
    ys = _experts(ge, gs, gn, xs, moe_w_gu[l], moe_b_gu[l], moe_w_dn[l], moe_b_dn[l])
    out = _combine(dest, ys, x1, gate_cols, gt_f, final_g, T)
    return out.reshape(B, T, D)
```

```python
import functools

import jax
import jax.numpy as jnp
from jax import lax
from jax.experimental import pallas as pl
from jax.experimental.pallas import tpu as pltpu

F32 = jnp.float32
BF16 = jnp.bfloat16
HIGHEST = lax.Precision.HIGHEST

D_MODEL = 2048
CHUNK = 64
RMS_EPS = 1e-5
RWKV_DIM = 1024
RWKV_HEAD = 64
DECAY_LORA = 64
A_LORA = 64
G_LORA = 160
GN_EPS = 64e-5
MLSTM_DIM = 1024
MLSTM_HEADS = 4
MLSTM_HEAD = 256
CONV_W = 4
HEAD_NORM_EPS = 1e-6
N_EXPERTS = 32
TOP_K = 4
D_FF = 2048
SWIGLU_LIMIT = 7.0
SWIGLU_ALPHA = 1.702
MOE_BLOCK = 256

COL_R, COL_K, COL_V, COL_Q, COL_MK, COL_MV, COL_MO = (i * 1024 for i in range(7))
COL_GL = 7168
COL_LORA = 7424
COL_IF = 7552
IN_COLS_P = 7680
QUAD = 256

VMEM_LIMIT = 56 * 1024 * 1024


def _cparams(sem, vmem=VMEM_LIMIT):
    return pltpu.CompilerParams(dimension_semantics=sem, vmem_limit_bytes=vmem)


def _dot(a, b, **kw):
    return jnp.dot(a, b, preferred_element_type=F32, **kw)


def _dot_nt(a, b, **kw):
    return lax.dot_general(a, b, (((1,), (1,)), ((), ())), preferred_element_type=F32, **kw)


def _dot_tn(a, b, **kw):
    return lax.dot_general(a, b, (((0,), (0,)), ((), ())), preferred_element_type=F32, **kw)


def _split3(x):
    h = x.astype(BF16)
    r1 = x - h.astype(F32)
    m = r1.astype(BF16)
    l = (r1 - m.astype(F32)).astype(BF16)
    return h, m, l


def _iota(shape, dim):
    return lax.broadcasted_iota(jnp.int32, shape, dim)


def _adaln_kernel(c_ref, w_ref, b_ref, o_ref):
    c = c_ref[...]
    s = c * jax.nn.sigmoid(c)
    o_ref[...] = _dot(s, w_ref[...], precision=HIGHEST) + b_ref[...]


def _adaln(c, ada_w, ada_b):
    B, D = c.shape
    n_out = ada_w.shape[1]
    tn = 1024
    c8 = jnp.zeros((8, D), F32).at[:B].set(c)
    out = pl.pallas_call(
        _adaln_kernel,
        grid=(n_out // tn,),
        in_specs=[pl.BlockSpec((8, D), lambda j: (0, 0)),
                  pl.BlockSpec((D, tn), lambda j: (0, j)),
                  pl.BlockSpec((1, tn), lambda j: (0, j))],
        out_specs=pl.BlockSpec((8, tn), lambda j: (0, j)),
        out_shape=jax.ShapeDtypeStruct((8, n_out), F32),
        compiler_params=_cparams(("parallel",)),
        name="adaln",
    )(c8, ada_w, ada_b.reshape(1, n_out))
    return out[:B]


def _inproj_kernel(x_ref, g_ref, sc_ref, sh_ref, w_ref, o_ref, h_scr):
    @pl.when(pl.program_id(1) == 0)
    def _():
        x = x_ref[...]
        y = x * lax.rsqrt(jnp.mean(x * x, axis=-1, keepdims=True) + RMS_EPS)
        h = (y * g_ref[...]) * (1.0 + sc_ref[0]) + sh_ref[0]
        h_scr[...] = h.astype(BF16)

    o_ref[...] = _dot(h_scr[...], w_ref[...])


def _in_proj(x2, norm_g, sc, sh, w_p, seq):
    N, D = x2.shape
    tm, tn = 1024, 768
    tiles_per_b = seq // tm
    ncols = w_p.shape[1]
    return pl.pallas_call(
        _inproj_kernel,
        grid=(N // tm, ncols // tn),
        in_specs=[pl.BlockSpec((tm, D), lambda i, j: (i, 0)),
                  pl.BlockSpec((1, D), lambda i, j: (0, 0)),
                  pl.BlockSpec((1, 1, D), lambda i, j: (i // tiles_per_b, 0, 0)),
                  pl.BlockSpec((1, 1, D), lambda i, j: (i // tiles_per_b, 0, 0)),
                  pl.BlockSpec((D, tn), lambda i, j: (0, j))],
        out_specs=pl.BlockSpec((tm, tn), lambda i, j: (i, j)),
        out_shape=jax.ShapeDtypeStruct((N, ncols), F32),
        scratch_shapes=[pltpu.VMEM((tm, D), BF16)],
        compiler_params=_cparams(("parallel", "arbitrary")),
        name="in_proj",
    )(x2, norm_g.reshape(1, D), sc, sh, w_p)


def _shift_rows(x, prev_row):
    rolled = pltpu.roll(x, 1, axis=0)
    row = _iota(x.shape, 0)
    return jnp.where(row == 0, prev_row, rolled)


def _rwkv_kernel(r_ref, k_ref, v_ref, gl_ref, lo_ref,
                 mu_r_ref, mu_k_ref, mu_v_ref, mu_gl_ref, mu_lo_ref,
                 w0_ref, a0_ref, kkp_ref, kap_ref, rk_ref, lnw_ref, lnb_ref,
                 w2_ref, a2_ref, g2_ref,
                 o_ref,
                 s_scr, prev_scr):
    L = CHUNK
    c = pl.program_id(1)

    @pl.when(c == 0)
    def _():
        s_scr[...] = jnp.zeros_like(s_scr)
        prev_scr[...] = jnp.zeros_like(prev_scr)

    def lerp(ref, mu_ref, lo, width):
        x = ref[...]
        prev = prev_scr[0:1, lo:lo + width]
        out = x + (_shift_rows(x, prev) - x) * mu_ref[...]
        prev_scr[0:1, lo:lo + width] = x[L - 1:L, :]
        return out

    r = lerp(r_ref, mu_r_ref, 0, RWKV_DIM)
    k = lerp(k_ref, mu_k_ref, 1024, RWKV_DIM)
    v = lerp(v_ref, mu_v_ref, 2048, RWKV_DIM)
    gl = lerp(gl_ref, mu_gl_ref, 3072, 256)
    lo = lerp(lo_ref, mu_lo_ref, 3328, 128)

    w = -jax.nn.softplus(-(w0_ref[...] + _dot(jnp.tanh(lo), w2_ref[...], precision=HIGHEST))) - 0.5
    lw = -jnp.exp(w)
    a = jax.nn.sigmoid(a0_ref[...] + _dot(lo, a2_ref[...], precision=HIGHEST))
    g = _dot(jax.nn.sigmoid(gl).astype(BF16), g2_ref[...])
    kkraw = k * kkp_ref[...]
    k2 = k * (1.0 + (a - 1.0) * kap_ref[...])

    ltri = jnp.where(_iota((L, L), 1) <= _iota((L, L), 0), 1.0, 0.0).astype(BF16)
    qr = _iota((QUAD, QUAD), 0)
    qc = _iota((QUAD, QUAD), 1)
    bd_mask = (qr // RWKV_HEAD) == (qc // RWKV_HEAD)
    bd = jnp.where(bd_mask, 1.0, 0.0).astype(BF16)
    strict_bd = bd_mask & ((qc % L) < (qr % L))
    incl_bd = bd_mask & ((qc % L) <= (qr % L))
    lane_head = _iota((1, QUAD), 1) // RWKV_HEAD
    n_heads = QUAD // RWKV_HEAD
    quads = range(RWKV_DIM // QUAD)

    def head_sum(x):
        h, m, l = _split3(x)
        return _dot(h, bd) + _dot(m, bd) + _dot(l, bd)

    def time_cumsum(x):
        h, m, l = _split3(x)
        return _dot(ltri, h) + _dot(ltri, m) + _dot(ltri, l)

    def stack_heads(x):
        return jnp.concatenate([jnp.where(lane_head == h, x, jnp.zeros_like(x)) for h in range(n_heads)], axis=0)

    def unstack_heads(x):
        out = x[0:L]
        for h in range(1, n_heads):
            out = out + x[h * L:(h + 1) * L]
        return out

    sls = [slice(q * QUAD, (q + 1) * QUAD) for q in quads]

    kk = []
    for sl in sls:
        kkr = kkraw[:, sl]
        kk.append(kkr / jnp.maximum(jnp.sqrt(head_sum(kkr * kkr)), 1e-12))
    cl = [time_cumsum(lw[:, sl]) for sl in sls]

    at, rt, bt, kt, p_last = [], [], [], [], []
    for q, sl in enumerate(sls):
        p_t = jnp.exp(cl[q])
        p_prev = jnp.exp(cl[q] - lw[:, sl])
        p_inv = jnp.exp(-cl[q])
        at.append((-kk[q] * p_prev).astype(BF16))
        rt.append((r[:, sl] * p_t).astype(BF16))
        bt.append((kk[q] * a[:, sl] * p_inv).astype(BF16))
        kt.append((k2[:, sl] * p_inv).astype(BF16))
        p_last.append(p_t[L - 1:L, :])

    s0 = [s_scr[q] for q in quads]
    n_pow, a_ak, a_rb, a_rk, rb4, rk4, vst, xs_st, y_state = [], [], [], [], [], [], [], [], []
    for q, sl in enumerate(sls):
        lst = jnp.concatenate([stack_heads(at[q]), stack_heads(rt[q])], axis=0)
        rb4.append(jnp.concatenate([bt[q]] * n_heads, axis=0))
        rk4.append(jnp.concatenate([kt[q]] * n_heads, axis=0))
        gb = _dot_nt(lst, rb4[q])
        gk = _dot_nt(lst, rk4[q])
        n_pow.append(jnp.where(strict_bd, gb[:QUAD], 0.0))
        a_rb.append(jnp.where(incl_bd, gb[QUAD:], 0.0).astype(BF16))
        a_ak.append(jnp.where(strict_bd, gk[:QUAD], 0.0).astype(BF16))
        a_rk.append(jnp.where(incl_bd, gk[QUAD:], 0.0).astype(BF16))
        vst.append(stack_heads(v[:, sl].astype(BF16)))
        xy = _dot_nt(jnp.concatenate([at[q], rt[q]], axis=0), s0[q].astype(BF16))
        xs_st.append(stack_heads(xy[:L]))
        y_state.append(xy[L:])

    u = [xs_st[q] + _dot(a_ak[q], vst[q]) for q in quads]
    for j in range(6):
        nb = [n_pow[q].astype(BF16) for q in quads]
        u = [u[q] + _dot(nb[q], u[q].astype(BF16)) for q in quads]
        if j < 5:
            n_pow = [_dot(nb[q], nb[q]) for q in quads]

    ub = [u[q].astype(BF16) for q in quads]
    y_st = [_dot(a_rb[q], ub[q]) + _dot(a_rk[q], vst[q]) for q in quads]
    for q in quads:
        z = s0[q] + _dot_tn(ub[q], rb4[q]) + _dot_tn(vst[q], rk4[q])
        s_scr[q] = jnp.where(bd_mask, z, 0.0) * p_last[q]

    inv_n = 1.0 / RWKV_HEAD
    y = [y_state[q] + unstack_heads(y_st[q]) for q in quads]
    mean = [head_sum(y[q]) * inv_n for q in quads]
    d = [y[q] - mean[q] for q in quads]
    var = [head_sum(d[q] * d[q]) * inv_n for q in quads]
    bonus = [head_sum(r[:, sl] * k2[:, sl] * rk_ref[:, sl]) * v[:, sl] for sl in sls]
    for q, sl in enumerate(sls):
        yn = d[q] * lax.rsqrt(var[q] + GN_EPS) * lnw_ref[:, sl] + lnb_ref[:, sl]
        o_ref[:, sl] = ((yn + bonus[q]) * g[:, sl]).astype(o_ref.dtype)


def _rwkv(proj, B, T, pr):
    NC = T // CHUNK
    L = CHUNK
    row = lambda b, c: b * NC + c

    def pspec(width, colblk):
        return pl.BlockSpec((L, width), lambda b, c: (row(b, c), colblk))

    def vec(width):
        return pl.BlockSpec((1, width), lambda b, c: (0, 0))

    def mat(rows):
        return pl.BlockSpec((rows, RWKV_DIM), lambda b, c: (0, 0))

    return pl.pallas_call(
        _rwkv_kernel,
        grid=(B, NC),
        in_specs=[pspec(1024, COL_R // 1024), pspec(1024, COL_K // 1024), pspec(1024, COL_V // 1024),
                  pspec(256, COL_GL // 256), pspec(128, COL_LORA // 128),
                  vec(1024), vec(1024), vec(1024), vec(256), vec(128),
                  vec(1024), vec(1024), vec(1024), vec(1024), vec(1024), vec(1024), vec(1024),
                  mat(128), mat(128), mat(256)],
        out_specs=pl.BlockSpec((L, RWKV_DIM), lambda b, c: (row(b, c), 0)),
        out_shape=jax.ShapeDtypeStruct((B * T, RWKV_DIM), BF16),
        scratch_shapes=[pltpu.VMEM((RWKV_DIM // QUAD, QUAD, QUAD), F32),
                        pltpu.VMEM((8, 3072 + 256 + 128), F32)],
        compiler_params=_cparams(("parallel", "arbitrary")),
        name="rwkv",
    )(proj, proj, proj, proj, proj,
      pr["mu_r"], pr["mu_k"], pr["mu_v"], pr["mu_gl"], pr["mu_lo"],
      pr["w0"], pr["a0"], pr["kk"], pr["ka"], pr["rk"], pr["ln_w"], pr["ln_b"],
      pr["w2"], pr["a2"], pr["g2"])


def _mlstm_kernel(q_ref, k_ref, v_ref, o_ref, gt_ref,
                  cwq_ref, cwk_ref, cbq_ref, cbk_ref, gbias_ref, ng_ref,
                  out_ref,
                  c_scr, n_scr, m_scr, prevq_scr, prevk_scr):
    L = CHUNK
    c = pl.program_id(1)

    @pl.when(c == 0)
    def _():
        c_scr[...] = jnp.zeros_like(c_scr)
        n_scr[...] = jnp.zeros_like(n_scr)
        m_scr[...] = jnp.zeros_like(m_scr)
        prevq_scr[...] = jnp.zeros_like(prevq_scr)
        prevk_scr[...] = jnp.zeros_like(prevk_scr)

    def conv_silu(ref, prev_scr, cw_ref, cb_ref):
        x = ref[...]
        prev8 = prev_scr[...]
        row8 = _iota(prev8.shape, 0)
        out = cb_ref[...] + x * cw_ref[CONV_W - 1:CONV_W, :]
        for d in range(1, CONV_W):
            rolled = pltpu.roll(x, d, axis=0)
            first8 = jnp.where(row8 < d, pltpu.roll(prev8, d, axis=0), rolled[:8])
            xs = jnp.concatenate([first8, rolled[8:]], axis=0)
            out = out + xs * cw_ref[CONV_W - 1 - d:CONV_W - d, :]
        prev_scr[...] = x[L - 8:L, :]
        return out * jax.nn.sigmoid(out)

    qf = conv_silu(q_ref, prevq_scr, cwq_ref, cbq_ref)
    kf = conv_silu(k_ref, prevk_scr, cwk_ref, cbk_ref) * (MLSTM_HEAD ** -0.5)
    vf = v_ref[...]

    z = gt_ref[...] + gbias_ref[...]
    lane = _iota(z.shape, 1)
    gc = jnp.where(lane < MLSTM_HEADS, z,
                   jnp.where(lane < 2 * MLSTM_HEADS, jax.nn.log_sigmoid(z), 0.0))
    row = _iota((L, L), 0)
    col = _iota((L, L), 1)
    causal = col <= row
    ltri = jnp.where(causal, 1.0, 0.0).astype(BF16)
    gh, gm, glo = _split3(gc)
    cum = _dot(ltri, gh) + _dot(ltri, gm) + _dot(ltri, glo)
    gc_t = gc.T
    cum_t = cum.T

    for h in range(MLSTM_HEADS):
        sl = slice(h * MLSTM_HEAD, (h + 1) * MLSTM_HEAD)
        qh, kh, vh = qf[:, sl], kf[:, sl], vf[:, sl]
        li_row = gc_t[h:h + 1, :]
        li_col = gc[:, h:h + 1]
        b_row = cum_t[MLSTM_HEADS + h:MLSTM_HEADS + h + 1, :]
        b_col = cum[:, MLSTM_HEADS + h:MLSTM_HEADS + h + 1]
        m_prev = m_scr[h:h + 1, 0:1]
        c_prev = c_scr[h]
        n_prev = n_scr[h:h + 1, :]

        a_inter = b_col + m_prev
        dm = jnp.where(causal, b_col - b_row + li_row, -jnp.inf)
        m_t = jnp.maximum(a_inter, jnp.max(dm, axis=-1, keepdims=True))
        w_inter = jnp.exp(a_inter - m_t)
        wmat = jnp.exp(dm - m_t)
        qb, kb, vb = qh.astype(BF16), kh.astype(BF16), vh.astype(BF16)
        s = _dot_nt(qb, kb) * wmat
        num = w_inter * _dot_nt(qb, c_prev.astype(BF16)) + _dot(s.astype(BF16), vb)
        den = w_inter * jnp.sum(qh * n_prev, axis=-1, keepdims=True) + jnp.sum(s, axis=-1, keepdims=True)
        hh = num / jnp.maximum(jnp.abs(den), jnp.exp(-m_t))

        m_new = m_t[L - 1:L, :]
        b_last = b_col[L - 1:L, :]
        g_state = jnp.exp(b_last + m_prev - m_new)
        w_s = jnp.exp(b_last - b_col + li_col - m_new)
        c_scr[h] = g_state * c_prev + _dot_tn((vh * w_s).astype(BF16), kb)
        n_scr[h:h + 1, :] = g_state * n_prev + jnp.sum(kh * w_s, axis=0, keepdims=True)
        m_scr[h:h + 1, :] = jnp.broadcast_to(m_new, (1, m_scr.shape[1]))

        hn = hh * lax.rsqrt(jnp.mean(hh * hh, axis=-1, keepdims=True) + HEAD_NORM_EPS)
        out_ref[:, sl] = ((hn * ng_ref[:, sl]) * jax.nn.sigmoid(o_ref[:, sl])).astype(out_ref.dtype)


def _mlstm(proj, B, T, pr):
    NC = T // CHUNK
    L = CHUNK
    row = lambda b, c: b * NC + c

    def pspec(width, colblk):
        return pl.BlockSpec((L, width), lambda b, c: (row(b, c), colblk))

    def cst(shape):
        return pl.BlockSpec(shape, lambda b, c: (0, 0))

    return pl.pallas_call(
        _mlstm_kernel,
        grid=(B, NC),
        in_specs=[pspec(1024, COL_Q // 1024), pspec(1024, COL_MK // 1024), pspec(1024, COL_MV // 1024),
                  pspec(1024, COL_MO // 1024), pspec(128, COL_IF // 128),
                  cst((CONV_W, 1024)), cst((CONV_W, 1024)), cst((1, 1024)), cst((1, 1024)),
                  cst((1, 128)), cst((1, 1024))],
        out_specs=pl.BlockSpec((L, MLSTM_DIM), lambda b, c: (row(b, c), 0)),
        out_shape=jax.ShapeDtypeStruct((B * T, MLSTM_DIM), BF16),
        scratch_shapes=[pltpu.VMEM((MLSTM_HEADS, MLSTM_HEAD, MLSTM_HEAD), F32),
                        pltpu.VMEM((8, MLSTM_HEAD), F32),
                        pltpu.VMEM((8, 128), F32),
                        pltpu.VMEM((8, 1024), F32),
                        pltpu.VMEM((8, 1024), F32)],
        compiler_params=_cparams(("parallel", "arbitrary")),
        name="mlstm",
    )(proj, proj, proj, proj, proj,
      pr["cw_q"], pr["cw_k"], pr["cb_q"], pr["cb_k"], pr["gbias"], pr["norm_g"])


def _out_route_kernel(yr_ref, ym_ref, x_ref, wo_ref, gt_ref, g2_ref, sc_ref, sh_ref, rw_ref, rb_ref,
                      x1_ref, h2_ref, idx_ref, gate_ref, rank_ref, cnt_ref,
                      carry_scr):
    i = pl.program_id(0)
    tm = x_ref.shape[0]

    @pl.when(i == 0)
    def _():
        carry_scr[...] = jnp.zeros_like(carry_scr)

    mix = _dot(yr_ref[...], wo_ref[0:RWKV_DIM, :]) + _dot(ym_ref[...], wo_ref[RWKV_DIM:, :])
    x1 = x_ref[...] + gt_ref[0] * mix
    x1_ref[...] = x1
    y = x1 * lax.rsqrt(jnp.mean(x1 * x1, axis=-1, keepdims=True) + RMS_EPS)
    h2 = (y * g2_ref[...]) * (1.0 + sc_ref[0]) + sh_ref[0]
    hbits = lax.bitcast_convert_type(h2.astype(BF16).astype(F32), jnp.uint32)
    half = h2.shape[1] // 2
    h2_ref[...] = (hbits[:, :half] >> 16) | (hbits[:, half:] & jnp.uint32(0xFFFF0000))

    logits = _dot(h2, rw_ref[...], precision=HIGHEST) + rb_ref[...]
    lt = logits.T[:N_EXPERTS, :]
    e_iota = _iota(lt.shape, 0)
    onehots, vals, idxs = [], [], []
    for _ in range(TOP_K):
        mx = jnp.max(lt, axis=0, keepdims=True)
        idx = jnp.min(jnp.where(lt == mx, e_iota, N_EXPERTS), axis=0, keepdims=True)
        sel = e_iota == idx
        onehots.append(sel)
        vals.append(mx)
        idxs.append(idx)
        lt = jnp.where(sel, -jnp.inf, lt)
    exps = [jnp.exp(vv - vals[0]) for vv in vals]
    denom = exps[0] + exps[1] + exps[2] + exps[3]
    gates = [e / denom for e in exps]

    member = jnp.zeros(onehots[0].shape, F32)
    for sel in onehots:
        member = member + jnp.where(sel, 1.0, 0.0)
    ur = _iota((tm, tm), 0)
    uc = _iota((tm, tm), 1)
    ustrict = jnp.where(ur < uc, 1.0, 0.0).astype(BF16)
    before = _dot(member.astype(BF16), ustrict) + carry_scr[:, 0:1]
    for j in range(TOP_K):
        rank = jnp.sum(jnp.where(onehots[j], before, 0.0), axis=0, keepdims=True)
        rank_ref[j:j + 1, :] = rank.astype(jnp.int32)
        idx_ref[j:j + 1, :] = idxs[j]
    new_carry = carry_scr[...] + jnp.sum(member, axis=1, keepdims=True)
    carry_scr[...] = new_carry
    cnt_ref[...] = new_carry.astype(jnp.int32)

    grows = jnp.concatenate(gates + [jnp.zeros((128 - TOP_K, tm), F32)], axis=0)
    gate_ref[...] = grows.T


def _out_route(yr, ym, x2, w_out_b, gt, g2, sc, sh, rw_p, rb_p, seq):
    N, D = x2.shape
    tm = 512
    tiles_per_b = seq // tm
    bvec = pl.BlockSpec((1, 1, D), lambda i: (i // tiles_per_b, 0, 0))
    return pl.pallas_call(
        _out_route_kernel,
        grid=(N // tm,),
        in_specs=[pl.BlockSpec((tm, RWKV_DIM), lambda i: (i, 0)),
                  pl.BlockSpec((tm, MLSTM_DIM), lambda i: (i, 0)),
                  pl.BlockSpec((tm, D), lambda i: (i, 0)),
                  pl.BlockSpec((D, D), lambda i: (0, 0)),
                  bvec,
                  pl.BlockSpec((1, D), lambda i: (0, 0)),
                  bvec, bvec,
                  pl.BlockSpec((D, 128), lambda i: (0, 0)),
                  pl.BlockSpec((1, 128), lambda i: (0, 0))],
        out_specs=[pl.BlockSpec((tm, D), lambda i: (i, 0)),
                   pl.BlockSpec((tm, D // 2), lambda i: (i, 0)),
                   pl.BlockSpec((TOP_K, tm), lambda i: (0, i)),
                   pl.BlockSpec((tm, 128), lambda i: (i, 0)),
                   pl.BlockSpec((TOP_K, tm), lambda i: (0, i)),
                   pl.BlockSpec((N_EXPERTS, 128), lambda i: (0, 0))],
        out_shape=[jax.ShapeDtypeStruct((N, D), F32),
                   jax.ShapeDtypeStruct((N, D // 2), jnp.uint32),
                   jax.ShapeDtypeStruct((TOP_K, N), jnp.int32),
                   jax.ShapeDtypeStruct((N, 128), F32),
                   jax.ShapeDtypeStruct((TOP_K, N), jnp.int32),
                   jax.ShapeDtypeStruct((N_EXPERTS, 128), jnp.int32)],
        scratch_shapes=[pltpu.VMEM((N_EXPERTS, 128), F32)],
        compiler_params=_cparams(("arbitrary",)),
        name="out_route",
    )(yr, ym, x2, w_out_b, gt, g2.reshape(1, D), sc, sh, rw_p, rb_p)


def _dest_kernel(idx_ref, rank_ref, cnt_ref, dest_ref):
    cnt = cnt_ref[:, 0:1].astype(F32)
    padded = jnp.floor((cnt + (MOE_BLOCK - 1)) * (1.0 / MOE_BLOCK)) * MOE_BLOCK
    er = _iota((N_EXPERTS, N_EXPERTS), 0)
    ec = _iota((N_EXPERTS, N_EXPERTS), 1)
    before = jnp.where(ec < er, 1.0, 0.0)
    pstart = _dot(before, jnp.broadcast_to(padded, (N_EXPERTS, 128)), precision=HIGHEST)[:, 0:1]
    e_iota = _iota((N_EXPERTS, idx_ref.shape[1]), 0)
    for j in range(TOP_K):
        sel = e_iota == idx_ref[j:j + 1, :]
        base = jnp.sum(jnp.where(sel, pstart, 0.0), axis=0, keepdims=True)
        dest_ref[j:j + 1, :] = base.astype(jnp.int32) + rank_ref[j:j + 1, :]


def _dest(idx, rank, cnt):
    N = idx.shape[1]
    tn = 2048
    return pl.pallas_call(
        _dest_kernel,
        grid=(N // tn,),
        in_specs=[pl.BlockSpec((TOP_K, tn), lambda i: (0, i)),
                  pl.BlockSpec((TOP_K, tn), lambda i: (0, i)),
                  pl.BlockSpec((N_EXPERTS, 128), lambda i: (0, 0))],
        out_specs=pl.BlockSpec((TOP_K, tn), lambda i: (0, i)),
        out_shape=jax.ShapeDtypeStruct((TOP_K, N), jnp.int32),
        compiler_params=_cparams(("parallel",)),
        name="dest",
    )(idx, rank, cnt)


def _dispatch_kernel(dest_ref, h_ref, xs_in_ref, xs_ref, sem):
    del xs_in_ref
    i = pl.program_id(0)
    tm = h_ref.shape[0]
    n_tok = pl.num_programs(0) * tm

    def copy(t, j):
        d = dest_ref[j * n_tok + i * tm + t]
        return pltpu.make_async_copy(h_ref.at[pl.ds(t, 1)], xs_ref.at[pl.ds(d, 1)], sem)

    def issue(t, carry):
        for j in range(TOP_K):
            copy(t, j).start()
        return carry

    lax.fori_loop(0, tm, issue, 0)

    def drain(t, carry):
        for j in range(TOP_K):
            copy(t, j).wait()
        return carry

    lax.fori_loop(0, tm, drain, 0)


def _dispatch(dest_flat, h2, n_rows):
    N, D = h2.shape
    tm = 256
    xs0 = jnp.zeros((n_rows, D), h2.dtype)
    return pl.pallas_call(
        _dispatch_kernel,
        grid_spec=pltpu.PrefetchScalarGridSpec(
            num_scalar_prefetch=1,
            grid=(N // tm,),
            in_specs=[pl.BlockSpec((tm, D), lambda i, d: (i, 0)),
                      pl.BlockSpec(memory_space=pl.ANY)],
            out_specs=pl.BlockSpec(memory_space=pl.ANY),
            scratch_shapes=[pltpu.SemaphoreType.DMA(())]),
        out_shape=jax.ShapeDtypeStruct((n_rows, D), h2.dtype),
        input_output_aliases={2: 0},
        compiler_params=_cparams(("arbitrary",)),
        name="dispatch",
    )(dest_flat, h2, xs0)


GROUP_ROWS = 2048
FF_TILE = 256
FFN_SPAN = 4


def _experts_kernel(ge_ref, gs_ref, gn_ref,
                    xs_ref, wg_ref, wu_ref, wd_ref, bg_ref, bu_ref, bd_ref,
                    ys_ref,
                    xbuf, xb16, acc, wgb, wub, wdb, sem):
    g = pl.program_id(0)
    j = pl.program_id(1)
    nblk = gn_ref[g]
    row0 = pl.multiple_of(gs_ref[g] * MOE_BLOCK, MOE_BLOCK)

    def in_copy(s):
        off = pl.multiple_of(s * MOE_BLOCK, MOE_BLOCK)
        return pltpu.make_async_copy(xs_ref.at[pl.ds(row0 + off, MOE_BLOCK)],
                                     xbuf.at[pl.ds(off, MOE_BLOCK)], sem.at[0])

    def out_copy(s):
        off = pl.multiple_of(s * MOE_BLOCK, MOE_BLOCK)
        return pltpu.make_async_copy(acc.at[pl.ds(off, MOE_BLOCK)],
                                     ys_ref.at[pl.ds(row0 + off, MOE_BLOCK)], sem.at[1])

    def for_blocks(fn):
        def body(s, carry):
            fn(s)
            return carry
        lax.fori_loop(0, nblk, body, 0)

    @pl.when(nblk > 0)
    def _():
        @pl.when(j == 0)
        def _():
            for_blocks(lambda s: in_copy(s).start())
            for_blocks(lambda s: in_copy(s).wait())

        wgb[...] = wg_ref[0].astype(BF16)
        wub[...] = wu_ref[0].astype(BF16)
        wdb[...] = wd_ref[0].astype(BF16)

        @pl.when(j == 0)
        def _():
            def init(s):
                rows = pl.ds(pl.multiple_of(s * MOE_BLOCK, MOE_BLOCK), MOE_BLOCK)
                w = xbuf[rows, :]
                half = w.shape[1]
                xb16[rows, 0:half] = lax.bitcast_convert_type(w << 16, F32).astype(BF16)
                xb16[rows, half:] = lax.bitcast_convert_type(w & jnp.uint32(0xFFFF0000), F32).astype(BF16)
                acc[rows, :] = jnp.broadcast_to(bd_ref[0], (MOE_BLOCK, acc.shape[1]))

            for_blocks(init)

        def ffn(row_start, m):
            rows = pl.ds(pl.multiple_of(row_start, MOE_BLOCK), m)
            xb = xb16[rows, :]
            gate = _dot(xb, wgb[...]) + bg_ref[0]
            up = _dot(xb, wub[...]) + bu_ref[0]
            gate = jnp.minimum(gate, SWIGLU_LIMIT)
            up = jnp.clip(up, -SWIGLU_LIMIT, SWIGLU_LIMIT)
            act = (up + 1.0) * (gate * jax.nn.sigmoid(SWIGLU_ALPHA * gate))
            acc[rows, :] = acc[rows, :] + _dot(act.astype(BF16), wdb[...])

        n_span = nblk // FFN_SPAN

        def span_body(s, carry):
            ffn(s * (FFN_SPAN * MOE_BLOCK), FFN_SPAN * MOE_BLOCK)
            return carry

        lax.fori_loop(0, n_span, span_body, 0)

        def rest_body(s, carry):
            ffn(s * MOE_BLOCK, MOE_BLOCK)
            return carry

        lax.fori_loop(n_span * FFN_SPAN, nblk, rest_body, 0)

        @pl.when(j == pl.num_programs(1) - 1)
        def _():
            for_blocks(lambda s: out_copy(s).start())
            for_blocks(lambda s: out_copy(s).wait())

    @pl.when((g == pl.num_programs(0) - 1) & (j == pl.num_programs(1) - 1))
    def _():
        acc[0:MOE_BLOCK, :] = jnp.zeros((MOE_BLOCK, acc.shape[1]), acc.dtype)

        def tail_copy(s):
            off = pl.multiple_of(s * MOE_BLOCK, MOE_BLOCK)
            return pltpu.make_async_copy(acc.at[pl.ds(0, MOE_BLOCK)], ys_ref.at[pl.ds(off, MOE_BLOCK)], sem.at[1])

        def over_tail(fn):
            def body(s, carry):
                fn(s)
                return carry
            lax.fori_loop(gn_ref[pl.num_programs(0)], ys_ref.shape[0] // MOE_BLOCK, body, 0)

        over_tail(lambda s: tail_copy(s).start())
        over_tail(lambda s: tail_copy(s).wait())


def _experts(ge, gs, gn, xs, w_gu, b_gu, w_dn, b_dn):
    R = xs.shape[0]
    D = D_MODEL
    E = w_gu.shape[0]
    G = ge.shape[0]
    J = D_FF // FF_TILE

    def jj(g, j, gn_ref):
        return jnp.where(gn_ref[g] > 0, j, J - 1)

    return pl.pallas_call(
        _experts_kernel,
        grid_spec=pltpu.PrefetchScalarGridSpec(
            num_scalar_prefetch=3,
            grid=(G, J),
            in_specs=[pl.BlockSpec(memory_space=pl.ANY),
                      pl.BlockSpec((1, D, FF_TILE), lambda g, j, ge, gs, gn: (ge[g], 0, jj(g, j, gn))),
                      pl.BlockSpec((1, D, FF_TILE), lambda g, j, ge, gs, gn: (ge[g], 0, J + jj(g, j, gn))),
                      pl.BlockSpec((1, FF_TILE, D), lambda g, j, ge, gs, gn: (ge[g], jj(g, j, gn), 0)),
                      pl.BlockSpec((1, 1, FF_TILE), lambda g, j, ge, gs, gn: (ge[g], 0, jj(g, j, gn))),
                      pl.BlockSpec((1, 1, FF_TILE), lambda g, j, ge, gs, gn: (ge[g], 0, J + jj(g, j, gn))),
                      pl.BlockSpec((1, 1, D), lambda g, j, ge, gs, gn: (ge[g], 0, 0))],
            out_specs=pl.BlockSpec(memory_space=pl.ANY),
            scratch_shapes=[pltpu.VMEM((GROUP_ROWS, D // 2), jnp.uint32),
                            pltpu.VMEM((GROUP_ROWS, D), BF16),
                            pltpu.VMEM((GROUP_ROWS, D), F32),
                            pltpu.VMEM((D, FF_TILE), BF16),
                            pltpu.VMEM((D, FF_TILE), BF16),
                            pltpu.VMEM((FF_TILE, D), BF16),
                            pltpu.SemaphoreType.DMA((2,))]),
        out_shape=jax.ShapeDtypeStruct((R, D), F32),
        compiler_params=_cparams(("arbitrary", "arbitrary")),
        name="experts",
    )(ge, gs, gn, xs, w_gu, w_gu, w_dn, b_gu.reshape(E, 1, 2 * D_FF), b_gu.reshape(E, 1, 2 * D_FF),
      b_dn.reshape(E, 1, D))


def _combine_kernel(dest_ref, ys_ref, x1_ref, gate_ref, gt_ref, fg_ref, o_ref, ybuf, sem):
    i = pl.program_id(0)
    tm = x1_ref.shape[0]
    n_tok = pl.num_programs(0) * tm

    def copy(t, j):
        d = dest_ref[j * n_tok + i * tm + t]
        return pltpu.make_async_copy(ys_ref.at[pl.ds(d, 1)], ybuf.at[j, pl.ds(t, 1)], sem)

    def issue(t, carry):
        for j in range(TOP_K):
            copy(t, j).start()
        return carry

    lax.fori_loop(0, tm, issue, 0)

    def drain(t, carry):
        for j in range(TOP_K):
            copy(t, j).wait()
        return carry

    lax.fori_loop(0, tm, drain, 0)

    gts = gate_ref[...]
    y = ybuf[0] * gts[:, 0:1]
    for j in range(1, TOP_K):
        y = y + ybuf[j] * gts[:, j:j + 1]
    x2 = x1_ref[...] + gt_ref[0] * y
    o_ref[...] = (x2 * lax.rsqrt(jnp.mean(x2 * x2, axis=-1, keepdims=True) + RMS_EPS)) * fg_ref[...]


def _combine(dest_flat, ys, x1, gate_cols, gt, final_g, seq):
    N, D = x1.shape
    tm = 256
    tiles_per_b = seq // tm
    return pl.pallas_call(
        _combine_kernel,
        grid_spec=pltpu.PrefetchScalarGridSpec(
            num_scalar_prefetch=1,
            grid=(N // tm,),
            in_specs=[pl.BlockSpec(memory_space=pl.ANY),
                      pl.BlockSpec((tm, D), lambda i, d: (i, 0)),
                      pl.BlockSpec((tm, 128), lambda i, d: (i, 0)),
                      pl.BlockSpec((1, 1, D), lambda i, d: (i // tiles_per_b, 0, 0)),
                      pl.BlockSpec((1, D), lambda i, d: (0, 0))],
            out_specs=pl.BlockSpec((tm, D), lambda i, d: (i, 0)),
            scratch_shapes=[pltpu.VMEM((TOP_K, tm, D), F32),
                            pltpu.SemaphoreType.DMA(())]),
        out_shape=jax.ShapeDtypeStruct((N, D), F32),
        compiler_params=_cparams(("arbitrary",)),
        name="combine",
    )(dest_flat, ys, x1, gate_cols, gt, final_g.reshape(1, D))


def _relayout_w_in(w_in):
    D = w_in.shape[0]
    o = 0
    seg = {}
    for name, n in (("r", 1024), ("wl", 64), ("k", 1024), ("v", 1024), ("al", 64), ("gl", 160),
                    ("q", 1024), ("mk", 1024), ("mv", 1024), ("mo", 1024), ("i", 4), ("f", 4)):
        seg[name] = w_in[:, o:o + n]
        o += n
    z = lambda n: jnp.zeros((D, n), w_in.dtype)
    return jnp.concatenate(
        [seg["r"], seg["k"], seg["v"], seg["q"], seg["mk"], seg["mv"], seg["mo"],
         seg["gl"], z(96), seg["wl"], seg["al"], seg["i"], seg["f"], z(120)], axis=1).astype(BF16)


def kernel(x, c, ada_w, ada_b, norm1_g, w_in, rwkv_mu, rwkv_w0, rwkv_w2, rwkv_a0, rwkv_a2, rwkv_g2, rwkv_kk, rwkv_ka, rwkv_rk, rwkv_ln_w, rwkv_ln_b, mlstm_conv_w, mlstm_conv_b, mlstm_b_i, mlstm_b_f, mlstm_norm_g, w_out, norm2_g, router_w, router_b, moe_w_gu, moe_b_gu, moe_w_dn, moe_b_dn, final_g):
    B, T, D = x.shape
    N = B * T
    x2 = x.reshape(N, D)
    l = 0

    mod = _adaln(c, ada_w[l], ada_b[l])
    sh_m, sc_m, gt_m, sh_f, sc_f, gt_f = [m.reshape(B, 1, D) for m in jnp.split(mod, 6, axis=-1)]

    proj = _in_proj(x2, norm1_g[l], sc_m, sh_m, _relayout_w_in(w_in[l]), T)

    mu = rwkv_mu[l]
    mu_r, mu_wl, mu_k, mu_v, mu_al, mu_gl = (mu[0:1024], mu[1024:1088], mu[1088:2112], mu[2112:3136],
                                             mu[3136:3200], mu[3200:3360])
    row = lambda a: a.reshape(1, -1)
    zrows = lambda n: jnp.zeros((n, RWKV_DIM), F32)
    rw = {
        "mu_r": row(mu_r), "mu_k": row(mu_k), "mu_v": row(mu_v),
        "mu_gl": row(jnp.concatenate([mu_gl, jnp.zeros((96,), F32)])),
        "mu_lo": row(jnp.concatenate([mu_wl, mu_al])),
        "w0": row(rwkv_w0[l]), "a0": row(rwkv_a0[l]), "kk": row(rwkv_kk[l]), "ka": row(rwkv_ka[l]),
        "rk": row(rwkv_rk[l]), "ln_w": row(rwkv_ln_w[l]), "ln_b": row(rwkv_ln_b[l]),
        "w2": jnp.concatenate([rwkv_w2[l], zrows(A_LORA)], axis=0),
        "a2": jnp.concatenate([zrows(DECAY_LORA), rwkv_a2[l]], axis=0),
        "g2": jnp.concatenate([rwkv_g2[l], zrows(256 - G_LORA)], axis=0).astype(BF16),
    }
    y_rwkv = _rwkv(proj, B, T, rw)

    cw = mlstm_conv_w[l]
    cb = mlstm_conv_b[l]
    ml = {
        "cw_q": cw[:, :MLSTM_DIM], "cw_k": cw[:, MLSTM_DIM:],
        "cb_q": row(cb[:MLSTM_DIM]), "cb_k": row(cb[MLSTM_DIM:]),
        "gbias": row(jnp.concatenate([mlstm_b_i[l], mlstm_b_f[l], jnp.zeros((120,), F32)])),
        "norm_g": row(mlstm_norm_g[l]),
    }
    y_mlstm = _mlstm(proj, B, T, ml)

    rw_p = jnp.concatenate([router_w[l], jnp.zeros((D, 128 - N_EXPERTS), F32)], axis=1)
    rb_p = jnp.concatenate([router_b[l], jnp.zeros((128 - N_EXPERTS,), F32)]).reshape(1, 128)
    x1, h2, idx, gate_cols, rank, cnt = _out_route(
        y_rwkv, y_mlstm, x2, w_out[l].astype(BF16), gt_m, norm2_g[l], sc_f, sh_f, rw_p, rb_p, T)

    dest = _dest(idx, rank, cnt).reshape(-1)

    counts = cnt[:, 0]
    per = GROUP_ROWS // MOE_BLOCK
    nblk = (counts + MOE_BLOCK - 1) // MOE_BLOCK
    blk_start = jnp.cumsum(nblk) - nblk
    ngrp = (nblk + per - 1) // per
    gend = jnp.cumsum(ngrp)
    gstart = gend - ngrp
    n_groups = N_EXPERTS + (N * TOP_K) // GROUP_ROWS
    gid = jnp.arange(n_groups, dtype=jnp.int32)
    ge_raw = jnp.minimum(jnp.searchsorted(gend, gid, side="right"), N_EXPERTS - 1).astype(jnp.int32)
    valid = gid < gend[-1]
    last_e = jnp.minimum(jnp.searchsorted(gend, jnp.maximum(gend[-1] - 1, 0), side="right"),
                         N_EXPERTS - 1).astype(jnp.int32)
    ge = jnp.where(valid, ge_raw, last_e).astype(jnp.int32)
    within = gid - gstart[ge_raw]
    gs = jnp.where(valid, blk_start[ge_raw] + within * per, 0).astype(jnp.int32)
    gn = jnp.where(valid, jnp.minimum(nblk[ge_raw] - within * per, per), 0).astype(jnp.int32)
    gn = jnp.concatenate([gn, jnp.sum(nblk, keepdims=True).astype(jnp.int32)])

    n_rows = ((N * TOP_K) // MOE_BLOCK + N_EXPERTS) * MOE_BLOCK
    xs = _dispatch(dest, h2, n_rows)
    ys = _experts(ge, gs, gn, xs, moe_w_gu[l], moe_b_gu[l], moe_w_dn[l], moe_b_dn[l])
    out = _combine(dest, ys, x1, gate_cols, gt_f, final_g, T)
    return out.reshape(B, T, D)
```

```python
import functools

import jax
import jax.numpy as jnp
from jax import lax
from jax.experimental import pallas as pl
from jax.experimental.pallas import tpu as pltpu

F32 = jnp.float32
BF16 = jnp.bfloat16
HIGHEST = lax.Precision.HIGHEST

D_MODEL = 2048
CHUNK = 64
RMS_EPS = 1e-5
RWKV_DIM = 1024
RWKV_HEAD = 64
DECAY_LORA = 64
A_LORA = 64
G_LORA = 160
GN_EPS = 64e-5
MLSTM_DIM = 1024
MLSTM_HEADS = 4
MLSTM_HEAD = 256
CONV_W = 4
HEAD_NORM_EPS = 1e-6
N_EXPERTS = 32
TOP_K = 4
D_FF = 2048
SWIGLU_LIMIT = 7.0
SWIGLU_ALPHA = 1.702
MOE_BLOCK = 256

COL_R, COL_K, COL_V, COL_Q, COL_MK, COL_MV, COL_MO = (i * 1024 for i in range(7))
COL_GL = 7168
COL_LORA = 7424
COL_IF = 7552
IN_COLS_P = 7680
QUAD = 256

VMEM_LIMIT = 56 * 1024 * 1024


def _cparams(sem, vmem=VMEM_LIMIT):
    return pltpu.CompilerParams(dimension_semantics=sem, vmem_limit_bytes=vmem)


def _dot(a, b, **kw):
    return jnp.dot(a, b, preferred_element_type=F32, **kw)


def _dot_nt(a, b, **kw):
    return lax.dot_general(a, b, (((1,), (1,)), ((), ())), preferred_element_type=F32, **kw)


def _dot_tn(a, b, **kw):
    return lax.dot_general(a, b, (((0,), (0,)), ((), ())), preferred_element_type=F32, **kw)


def _split3(x):
    h = x.astype(BF16)
    r1 = x - h.astype(F32)
    m = r1.astype(BF16)
    l = (r1 - m.astype(F32)).astype(BF16)
    return h, m, l


def _iota(shape, dim):
    return lax.broadcasted_iota(jnp.int32, shape, dim)


def _adaln_kernel(c_ref, w_ref, b_ref, o_ref):
    c = c_ref[...]
    s = c * jax.nn.sigmoid(c)
    o_ref[...] = _dot(s, w_ref[...], precision=HIGHEST) + b_ref[...]


def _adaln(c, ada_w, ada_b):
    B, D = c.shape
    n_out = ada_w.shape[1]
    tn = 1024
    c8 = jnp.zeros((8, D), F32).at[:B].set(c)
    out = pl.pallas_call(
        _adaln_kernel,
        grid=(n_out // tn,),
        in_specs=[pl.BlockSpec((8, D), lambda j: (0, 0)),
                  pl.BlockSpec((D, tn), lambda j: (0, j)),
                  pl.BlockSpec((1, tn), lambda j: (0, j))],
        out_specs=pl.BlockSpec((8, tn), lambda j: (0, j)),
        out_shape=jax.ShapeDtypeStruct((8, n_out), F32),
        compiler_params=_cparams(("parallel",)),
        name="adaln",
    )(c8, ada_w, ada_b.reshape(1, n_out))
    return out[:B]


def _inproj_kernel(x_ref, g_ref, sc_ref, sh_ref, w_ref, o_ref, h_scr):
    @pl.when(pl.program_id(1) == 0)
    def _():
        x = x_ref[...]
        y = x * lax.rsqrt(jnp.mean(x * x, axis=-1, keepdims=True) + RMS_EPS)
        h = (y * g_ref[...]) * (1.0 + sc_ref[0]) + sh_ref[0]
        h_scr[...] = h.astype(BF16)

    o_ref[...] = _dot(h_scr[...], w_ref[...])


def _in_proj(x2, norm_g, sc, sh, w_p, seq):
    N, D = x2.shape
    tm, tn = 1024, 768
    tiles_per_b = seq // tm
    ncols = w_p.shape[1]
    return pl.pallas_call(
        _inproj_kernel,
        grid=(N // tm, ncols // tn),
        in_specs=[pl.BlockSpec((tm, D), lambda i, j: (i, 0)),
                  pl.BlockSpec((1, D), lambda i, j: (0, 0)),
                  pl.BlockSpec((1, 1, D), lambda i, j: (i // tiles_per_b, 0, 0)),
                  pl.BlockSpec((1, 1, D), lambda i, j: (i // tiles_per_b, 0, 0)),
                  pl.BlockSpec((D, tn), lambda i, j: (0, j))],
        out_specs=pl.BlockSpec((tm, tn), lambda i, j: (i, j)),
        out_shape=jax.ShapeDtypeStruct((N, ncols), F32),
        scratch_shapes=[pltpu.VMEM((tm, D), BF16)],
        compiler_params=_cparams(("parallel", "arbitrary")),
        name="in_proj",
    )(x2, norm_g.reshape(1, D), sc, sh, w_p)


def _shift_rows(x, prev_row):
    rolled = pltpu.roll(x, 1, axis=0)
    row = _iota(x.shape, 0)
    return jnp.where(row == 0, prev_row, rolled)


def _rwkv_kernel(r_ref, k_ref, v_ref, gl_ref, lo_ref,
                 mu_r_ref, mu_k_ref, mu_v_ref, mu_gl_ref, mu_lo_ref,
                 w0_ref, a0_ref, kkp_ref, kap_ref, rk_ref, lnw_ref, lnb_ref,
                 w2_ref, a2_ref, g2_ref,
                 o_ref,
                 s_scr, prev_scr):
    L = CHUNK
    c = pl.program_id(1)

    @pl.when(c == 0)
    def _():
        s_scr[...] = jnp.zeros_like(s_scr)
        prev_scr[...] = jnp.zeros_like(prev_scr)

    def lerp(ref, mu_ref, lo, width):
        x = ref[...]
        prev = prev_scr[0:1, lo:lo + width]
        out = x + (_shift_rows(x, prev) - x) * mu_ref[...]
        prev_scr[0:1, lo:lo + width] = x[L - 1:L, :]
        return out

    r = lerp(r_ref, mu_r_ref, 0, RWKV_DIM)
    k = lerp(k_ref, mu_k_ref, 1024, RWKV_DIM)
    v = lerp(v_ref, mu_v_ref, 2048, RWKV_DIM)
    gl = lerp(gl_ref, mu_gl_ref, 3072, 256)
    lo = lerp(lo_ref, mu_lo_ref, 3328, 128)

    w = -jax.nn.softplus(-(w0_ref[...] + _dot(jnp.tanh(lo), w2_ref[...], precision=HIGHEST))) - 0.5
    lw = -jnp.exp(w)
    a = jax.nn.sigmoid(a0_ref[...] + _dot(lo, a2_ref[...], precision=HIGHEST))
    g = _dot(jax.nn.sigmoid(gl).astype(BF16), g2_ref[...])
    kkraw = k * kkp_ref[...]
    k2 = k * (1.0 + (a - 1.0) * kap_ref[...])

    ltri = jnp.where(_iota((L, L), 1) <= _iota((L, L), 0), 1.0, 0.0).astype(BF16)
    qr = _iota((QUAD, QUAD), 0)
    qc = _iota((QUAD, QUAD), 1)
    bd_mask = (qr // RWKV_HEAD) == (qc // RWKV_HEAD)
    bd = jnp.where(bd_mask, 1.0, 0.0).astype(BF16)
    strict_bd = bd_mask & ((qc % L) < (qr % L))
    incl_bd = bd_mask & ((qc % L) <= (qr % L))
    lane_head = _iota((1, QUAD), 1) // RWKV_HEAD
    n_heads = QUAD // RWKV_HEAD
    quads = range(RWKV_DIM // QUAD)

    def head_sum(x):
        h, m, l = _split3(x)
        return _dot(h, bd) + _dot(m, bd) + _dot(l, bd)

    def time_cumsum(x):
        h, m, l = _split3(x)
        return _dot(ltri, h) + _dot(ltri, m) + _dot(ltri, l)

    def stack_heads(x):
        return jnp.concatenate([jnp.where(lane_head == h, x, jnp.zeros_like(x)) for h in range(n_heads)], axis=0)

    def unstack_heads(x):
        out = x[0:L]
        for h in range(1, n_heads):
            out = out + x[h * L:(h + 1) * L]
        return out

    sls = [slice(q * QUAD, (q + 1) * QUAD) for q in quads]

    kk = []
    for sl in sls:
        kkr = kkraw[:, sl]
        kk.append(kkr / jnp.maximum(jnp.sqrt(head_sum(kkr * kkr)), 1e-12))
    cl = [time_cumsum(lw[:, sl]) for sl in sls]

    at, rt, bt, kt, p_last = [], [], [], [], []
    for q, sl in enumerate(sls):
        p_t = jnp.exp(cl[q])
        p_prev = jnp.exp(cl[q] - lw[:, sl])
        p_inv = jnp.exp(-cl[q])
        at.append((-kk[q] * p_prev).astype(BF16))
        rt.append((r[:, sl] * p_t).astype(BF16))
        bt.append((kk[q] * a[:, sl] * p_inv).astype(BF16))
        kt.append((k2[:, sl] * p_inv).astype(BF16))
        p_last.append(p_t[L - 1:L, :])

    s0 = [s_scr[q] for q in quads]
    n_pow, a_ak, a_rb, a_rk, rb4, rk4, vst, xs_st, y_state = [], [], [], [], [], [], [], [], []
    for q, sl in enumerate(sls):
        lst = jnp.concatenate([stack_heads(at[q]), stack_heads(rt[q])], axis=0)
        rb4.append(jnp.concatenate([bt[q]] * n_heads, axis=0))
        rk4.append(jnp.concatenate([kt[q]] * n_heads, axis=0))
        gb = _dot_nt(lst, rb4[q])
        gk = _dot_nt(lst, rk4[q])
        n_pow.append(jnp.where(strict_bd, gb[:QUAD], 0.0))
        a_rb.append(jnp.where(incl_bd, gb[QUAD:], 0.0).astype(BF16))
        a_ak.append(jnp.where(strict_bd, gk[:QUAD], 0.0).astype(BF16))
        a_rk.append(jnp.where(incl_bd, gk[QUAD:], 0.0).astype(BF16))
        vst.append(stack_heads(v[:, sl].astype(BF16)))
        xy = _dot_nt(jnp.concatenate([at[q], rt[q]], axis=0), s0[q].astype(BF16))
        xs_st.append(stack_heads(xy[:L]))
        y_state.append(xy[L:])

    u = [xs_st[q] + _dot(a_ak[q], vst[q]) for q in quads]
    for j in range(6):
        nb = [n_pow[q].astype(BF16) for q in quads]
        u = [u[q] + _dot(nb[q], u[q].astype(BF16)) for q in quads]
        if j < 5:
            n_pow = [_dot(nb[q], nb[q]) for q in quads]

    ub = [u[q].astype(BF16) for q in quads]
    y_st = [_dot(a_rb[q], ub[q]) + _dot(a_rk[q], vst[q]) for q in quads]
    for q in quads:
        z = s0[q] + _dot_tn(ub[q], rb4[q]) + _dot_tn(vst[q], rk4[q])
        s_scr[q] = jnp.where(bd_mask, z, 0.0) * p_last[q]

    inv_n = 1.0 / RWKV_HEAD
    y = [y_state[q] + unstack_heads(y_st[q]) for q in quads]
    mean = [head_sum(y[q]) * inv_n for q in quads]
    d = [y[q] - mean[q] for q in quads]
    var = [head_sum(d[q] * d[q]) * inv_n for q in quads]
    bonus = [head_sum(r[:, sl] * k2[:, sl] * rk_ref[:, sl]) * v[:, sl] for sl in sls]
    for q, sl in enumerate(sls):
        yn = d[q] * lax.rsqrt(var[q] + GN_EPS) * lnw_ref[:, sl] + lnb_ref[:, sl]
        o_ref[:, sl] = ((yn + bonus[q]) * g[:, sl]).astype(o_ref.dtype)


def _rwkv(proj, B, T, pr):
    NC = T // CHUNK
    L = CHUNK
    row = lambda b, c: b * NC + c

    def pspec(width, colblk):
        return pl.BlockSpec((L, width), lambda b, c: (row(b, c), colblk))

    def vec(width):
        return pl.BlockSpec((1, width), lambda b, c: (0, 0))

    def mat(rows):
        return pl.BlockSpec((rows, RWKV_DIM), lambda b, c: (0, 0))

    return pl.pallas_call(
        _rwkv_kernel,
        grid=(B, NC),
        in_specs=[pspec(1024, COL_R // 1024), pspec(1024, COL_K // 1024), pspec(1024, COL_V // 1024),
                  pspec(256, COL_GL // 256), pspec(128, COL_LORA // 128),
                  vec(1024), vec(1024), vec(1024), vec(256), vec(128),
                  vec(1024), vec(1024), vec(1024), vec(1024), vec(1024), vec(1024), vec(1024),
                  mat(128), mat(128), mat(256)],
        out_specs=pl.BlockSpec((L, RWKV_DIM), lambda b, c: (row(b, c), 0)),
        out_shape=jax.ShapeDtypeStruct((B * T, RWKV_DIM), BF16),
        scratch_shapes=[pltpu.VMEM((RWKV_DIM // QUAD, QUAD, QUAD), F32),
                        pltpu.VMEM((8, 3072 + 256 + 128), F32)],
        compiler_params=_cparams(("parallel", "arbitrary")),
        name="rwkv",
    )(proj, proj, proj, proj, proj,
      pr["mu_r"], pr["mu_k"], pr["mu_v"], pr["mu_gl"], pr["mu_lo"],
      pr["w0"], pr["a0"], pr["kk"], pr["ka"], pr["rk"], pr["ln_w"], pr["ln_b"],
      pr["w2"], pr["a2"], pr["g2"])


def _mlstm_kernel(q_ref, k_ref, v_ref, o_ref, gt_ref,
                  cwq_ref, cwk_ref, cbq_ref, cbk_ref, gbias_ref, ng_ref,
                  out_ref,
                  c_scr, n_scr, m_scr, prevq_scr, prevk_scr):
    L = CHUNK
    c = pl.program_id(1)

    @pl.when(c == 0)
    def _():
        c_scr[...] = jnp.zeros_like(c_scr)
        n_scr[...] = jnp.zeros_like(n_scr)
        m_scr[...] = jnp.zeros_like(m_scr)
        prevq_scr[...] = jnp.zeros_like(prevq_scr)
        prevk_scr[...] = jnp.zeros_like(prevk_scr)

    def conv_silu(ref, prev_scr, cw_ref, cb_ref):
        x = ref[...]
        prev8 = prev_scr[...]
        row8 = _iota(prev8.shape, 0)
        out = cb_ref[...] + x * cw_ref[CONV_W - 1:CONV_W, :]
        for d in range(1, CONV_W):
            rolled = pltpu.roll(x, d, axis=0)
            first8 = jnp.where(row8 < d, pltpu.roll(prev8, d, axis=0), rolled[:8])
            xs = jnp.concatenate([first8, rolled[8:]], axis=0)
            out = out + xs * cw_ref[CONV_W - 1 - d:CONV_W - d, :]
        prev_scr[...] = x[L - 8:L, :]
        return out * jax.nn.sigmoid(out)

    qf = conv_silu(q_ref, prevq_scr, cwq_ref, cbq_ref)
    kf = conv_silu(k_ref, prevk_scr, cwk_ref, cbk_ref) * (MLSTM_HEAD ** -0.5)
    vf = v_ref[...]

    z = gt_ref[...] + gbias_ref[...]
    lane = _iota(z.shape, 1)
    gc = jnp.where(lane < MLSTM_HEADS, z,
                   jnp.where(lane < 2 * MLSTM_HEADS, jax.nn.log_sigmoid(z), 0.0))
    row = _iota((L, L), 0)
    col = _iota((L, L), 1)
    causal = col <= row
    ltri = jnp.where(causal, 1.0, 0.0).astype(BF16)
    gh, gm, glo = _split3(gc)
    cum = _dot(ltri, gh) + _dot(ltri, gm) + _dot(ltri, glo)
    gc_t = gc.T
    cum_t = cum.T

    for h in range(MLSTM_HEADS):
        sl = slice(h * MLSTM_HEAD, (h + 1) * MLSTM_HEAD)
        qh, kh, vh = qf[:, sl], kf[:, sl], vf[:, sl]
        li_row = gc_t[h:h + 1, :]
        li_col = gc[:, h:h + 1]
        b_row = cum_t[MLSTM_HEADS + h:MLSTM_HEADS + h + 1, :]
        b_col = cum[:, MLSTM_HEADS + h:MLSTM_HEADS + h + 1]
        m_prev = m_scr[h:h + 1, 0:1]
        c_prev = c_scr[h]
        n_prev = n_scr[h:h + 1, :]

        a_inter = b_col + m_prev
        dm = jnp.where(causal, b_col - b_row + li_row, -jnp.inf)
        m_t = jnp.maximum(a_inter, jnp.max(dm, axis=-1, keepdims=True))
        w_inter = jnp.exp(a_inter - m_t)
        wmat = jnp.exp(dm - m_t)
        qb, kb, vb = qh.astype(BF16), kh.astype(BF16), vh.astype(BF16)
        s = _dot_nt(qb, kb) * wmat
        num = w_inter * _dot_nt(qb, c_prev.astype(BF16)) + _dot(s.astype(BF16), vb)
        den = w_inter * jnp.sum(qh * n_prev, axis=-1, keepdims=True) + jnp.sum(s, axis=-1, keepdims=True)
        hh = num / jnp.maximum(jnp.abs(den), jnp.exp(-m_t))

        m_new = m_t[L - 1:L, :]
        b_last = b_col[L - 1:L, :]
        g_state = jnp.exp(b_last + m_prev - m_new)
        w_s = jnp.exp(b_last - b_col + li_col - m_new)
        c_scr[h] = g_state * c_prev + _dot_tn((vh * w_s).astype(BF16), kb)
        n_scr[h:h + 1, :] = g_state * n_prev + jnp.sum(kh * w_s, axis=0, keepdims=True)
        m_scr[h:h + 1, :] = jnp.broadcast_to(m_new, (1, m_scr.shape[1]))

        hn = hh * lax.rsqrt(jnp.mean(hh * hh, axis=-1, keepdims=True) + HEAD_NORM_EPS)
        out_ref[:, sl] = ((hn * ng_ref[:, sl]) * jax.nn.sigmoid(o_ref[:, sl])).astype(out_ref.dtype)


def _mlstm(proj, B, T, pr):
    NC = T // CHUNK
    L = CHUNK
    row = lambda b, c: b * NC + c

    def pspec(width, colblk):
        return pl.BlockSpec((L, width), lambda b, c: (row(b, c), colblk))

    def cst(shape):
        return pl.BlockSpec(shape, lambda b, c: (0, 0))

    return pl.pallas_call(
        _mlstm_kernel,
        grid=(B, NC),
        in_specs=[pspec(1024, COL_Q // 1024), pspec(1024, COL_MK // 1024), pspec(1024, COL_MV // 1024),
                  pspec(1024, COL_MO // 1024), pspec(128, COL_IF // 128),
                  cst((CONV_W, 1024)), cst((CONV_W, 1024)), cst((1, 1024)), cst((1, 1024)),
                  cst((1, 128)), cst((1, 1024))],
        out_specs=pl.BlockSpec((L, MLSTM_DIM), lambda b, c: (row(b, c), 0)),
        out_shape=jax.ShapeDtypeStruct((B * T, MLSTM_DIM), BF16),
        scratch_shapes=[pltpu.VMEM((MLSTM_HEADS, MLSTM_HEAD, MLSTM_HEAD), F32),
                        pltpu.VMEM((8, MLSTM_HEAD), F32),
                        pltpu.VMEM((8, 128), F32),
                        pltpu.VMEM((8, 1024), F32),
                        pltpu.VMEM((8, 1024), F32)],
        compiler_params=_cparams(("parallel", "arbitrary")),
        name="mlstm",
    )(proj, proj, proj, proj, proj,
      pr["cw_q"], pr["cw_k"], pr["cb_q"], pr["cb_k"], pr["gbias"], pr["norm_g"])


def _out_route_kernel(yr_ref, ym_ref, x_ref, wo_ref, gt_ref, g2_ref, sc_ref, sh_ref, rw_ref, rb_ref,
                      x1_ref, h2_ref, idx_ref, gate_ref, rank_ref, cnt_ref,
                      carry_scr):
    i = pl.program_id(0)
    tm = x_ref.shape[0]

    @pl.when(i == 0)
    def _():
        carry_scr[...] = jnp.zeros_like(carry_scr)

    mix = _dot(yr_ref[...], wo_ref[0:RWKV_DIM, :]) + _dot(ym_ref[...], wo_ref[RWKV_DIM:, :])
    x1 = x_ref[...] + gt_ref[0] * mix
    x1_ref[...] = x1
    y = x1 * lax.rsqrt(jnp.mean(x1 * x1, axis=-1, keepdims=True) + RMS_EPS)
    h2 = (y * g2_ref[...]) * (1.0 + sc_ref[0]) + sh_ref[0]
    hbits = lax.bitcast_convert_type(h2.astype(BF16).astype(F32), jnp.uint32)
    half = h2.shape[1] // 2
    h2_ref[...] = (hbits[:, :half] >> 16) | (hbits[:, half:] & jnp.uint32(0xFFFF0000))

    logits = _dot(h2, rw_ref[...], precision=HIGHEST) + rb_ref[...]
    lt = logits.T[:N_EXPERTS, :]
    e_iota = _iota(lt.shape, 0)
    onehots, vals, idxs = [], [], []
    for _ in range(TOP_K):
        mx = jnp.max(lt, axis=0, keepdims=True)
        idx = jnp.min(jnp.where(lt == mx, e_iota, N_EXPERTS), axis=0, keepdims=True)
        sel = e_iota == idx
        onehots.append(sel)
        vals.append(mx)
        idxs.append(idx)
        lt = jnp.where(sel, -jnp.inf, lt)
    exps = [jnp.exp(vv - vals[0]) for vv in vals]
    denom = exps[0] + exps[1] + exps[2] + exps[3]
    gates = [e / denom for e in exps]

    member = jnp.zeros(onehots[0].shape, F32)
    for sel in onehots:
        member = member + jnp.where(sel, 1.0, 0.0)
    ur = _iota((tm, tm), 0)
    uc = _iota((tm, tm), 1)
    ustrict = jnp.where(ur < uc, 1.0, 0.0).astype(BF16)
    before = _dot(member.astype(BF16), ustrict) + carry_scr[:, 0:1]
    for j in range(TOP_K):
        rank = jnp.sum(jnp.where(onehots[j], before, 0.0), axis=0, keepdims=True)
        rank_ref[j:j + 1, :] = rank.astype(jnp.int32)
        idx_ref[j:j + 1, :] = idxs[j]
    new_carry = carry_scr[...] + jnp.sum(member, axis=1, keepdims=True)
    carry_scr[...] = new_carry
    cnt_ref[...] = new_carry.astype(jnp.int32)

    grows = jnp.concatenate(gates + [jnp.zeros((128 - TOP_K, tm), F32)], axis=0)
    gate_ref[...] = grows.T


def _out_route(yr, ym, x2, w_out_b, gt, g2, sc, sh, rw_p, rb_p, seq):
    N, D = x2.shape
    tm = 512
    tiles_per_b = seq // tm
    bvec = pl.BlockSpec((1, 1, D), lambda i: (i // tiles_per_b, 0, 0))
    return pl.pallas_call(
        _out_route_kernel,
        grid=(N // tm,),
        in_specs=[pl.BlockSpec((tm, RWKV_DIM), lambda i: (i, 0)),
                  pl.BlockSpec((tm, MLSTM_DIM), lambda i: (i, 0)),
                  pl.BlockSpec((tm, D), lambda i: (i, 0)),
                  pl.BlockSpec((D, D), lambda i: (0, 0)),
                  bvec,
                  pl.BlockSpec((1, D), lambda i: (0, 0)),
                  bvec, bvec,
                  pl.BlockSpec((D, 128), lambda i: (0, 0)),
                  pl.BlockSpec((1, 128), lambda i: (0, 0))],
        out_specs=[pl.BlockSpec((tm, D), lambda i: (i, 0)),
                   pl.BlockSpec((tm, D // 2), lambda i: (i, 0)),
                   pl.BlockSpec((TOP_K, tm), lambda i: (0, i)),
                   pl.BlockSpec((tm, 128), lambda i: (i, 0)),
                   pl.BlockSpec((TOP_K, tm), lambda i: (0, i)),
                   pl.BlockSpec((N_EXPERTS, 128), lambda i: (0, 0))],
        out_shape=[jax.ShapeDtypeStruct((N, D), F32),
                   jax.ShapeDtypeStruct((N, D // 2), jnp.uint32),
                   jax.ShapeDtypeStruct((TOP_K, N), jnp.int32),
                   jax.ShapeDtypeStruct((N, 128), F32),
                   jax.ShapeDtypeStruct((TOP_K, N), jnp.int32),
                   jax.ShapeDtypeStruct((N_EXPERTS, 128), jnp.int32)],
        scratch_shapes=[pltpu.VMEM((N_EXPERTS, 128), F32)],
        compiler_params=_cparams(("arbitrary",)),
        name="out_route",
    )(yr, ym, x2, w_out_b, gt, g2.reshape(1, D), sc, sh, rw_p, rb_p)


def _dest_kernel(idx_ref, rank_ref, cnt_ref, dest_ref):
    cnt = cnt_ref[:, 0:1].astype(F32)
    padded = jnp.floor((cnt + (MOE_BLOCK - 1)) * (1.0 / MOE_BLOCK)) * MOE_BLOCK
    er = _iota((N_EXPERTS, N_EXPERTS), 0)
    ec = _iota((N_EXPERTS, N_EXPERTS), 1)
    before = jnp.where(ec < er, 1.0, 0.0)
    pstart = _dot(before, jnp.broadcast_to(padded, (N_EXPERTS, 128)), precision=HIGHEST)[:, 0:1]
    e_iota = _iota((N_EXPERTS, idx_ref.shape[1]), 0)
    for j in range(TOP_K):
        sel = e_iota == idx_ref[j:j + 1, :]
        base = jnp.sum(jnp.where(sel, pstart, 0.0), axis=0, keepdims=True)
        dest_ref[j:j + 1, :] = base.astype(jnp.int32) + rank_ref[j:j + 1, :]


def _dest(idx, rank, cnt):
    N = idx.shape[1]
    tn = 2048
    return pl.pallas_call(
        _dest_kernel,
        grid=(N // tn,),
        in_specs=[pl.BlockSpec((TOP_K, tn), lambda i: (0, i)),
                  pl.BlockSpec((TOP_K, tn), lambda i: (0, i)),
                  pl.BlockSpec((N_EXPERTS, 128), lambda i: (0, 0))],
        out_specs=pl.BlockSpec((TOP_K, tn), lambda i: (0, i)),
        out_shape=jax.ShapeDtypeStruct((TOP_K, N), jnp.int32),
        compiler_params=_cparams(("parallel",)),
        name="dest",
    )(idx, rank, cnt)


def _dispatch_kernel(dest_ref, h_ref, xs_in_ref, xs_ref, sem):
    del xs_in_ref
    i = pl.program_id(0)
    tm = h_ref.shape[0]
    n_tok = pl.num_programs(0) * tm

    def copy(t, j):
        d = dest_ref[j * n_tok + i * tm + t]
        return pltpu.make_async_copy(h_ref.at[pl.ds(t, 1)], xs_ref.at[pl.ds(d, 1)], sem)

    def issue(t, carry):
        for j in range(TOP_K):
            copy(t, j).start()
        return carry

    lax.fori_loop(0, tm, issue, 0)

    def drain(t, carry):
        for j in range(TOP_K):
            copy(t, j).wait()
        return carry

    lax.fori_loop(0, tm, drain, 0)


def _dispatch(dest_flat, h2, n_rows):
    N, D = h2.shape
    tm = 256
    xs0 = jnp.zeros((n_rows, D), h2.dtype)
    return pl.pallas_call(
        _dispatch_kernel,
        grid_spec=pltpu.PrefetchScalarGridSpec(
            num_scalar_prefetch=1,
            grid=(N // tm,),
            in_specs=[pl.BlockSpec((tm, D), lambda i, d: (i, 0)),
                      pl.BlockSpec(memory_space=pl.ANY)],
            out_specs=pl.BlockSpec(memory_space=pl.ANY),
            scratch_shapes=[pltpu.SemaphoreType.DMA(())]),
        out_shape=jax.ShapeDtypeStruct((n_rows, D), h2.dtype),
        input_output_aliases={2: 0},
        compiler_params=_cparams(("arbitrary",)),
        name="dispatch",
    )(dest_flat, h2, xs0)


GROUP_ROWS = 2048
FF_TILE = 256
FFN_SPAN = 4


def _experts_kernel(ge_ref, gs_ref, gn_ref,
                    xs_ref, wg_ref, wu_ref, wd_ref, bg_ref, bu_ref, bd_ref,
                    ys_ref,
                    xbuf, xb16, acc, wgb, wub, wdb, sem):
    g = pl.program_id(0)
    j = pl.program_id(1)
    nblk = gn_ref[g]
    row0 = pl.multiple_of(gs_ref[g] * MOE_BLOCK, MOE_BLOCK)

    def in_copy(s):
        off = pl.multiple_of(s * MOE_BLOCK, MOE_BLOCK)
        return pltpu.make_async_copy(xs_ref.at[pl.ds(row0 + off, MOE_BLOCK)],
                                     xbuf.at[pl.ds(off, MOE_BLOCK)], sem.at[0])

    def out_copy(s):
        off = pl.multiple_of(s * MOE_BLOCK, MOE_BLOCK)
        return pltpu.make_async_copy(acc.at[pl.ds(off, MOE_BLOCK)],
                                     ys_ref.at[pl.ds(row0 + off, MOE_BLOCK)], sem.at[1])

    def for_blocks(fn):
        def body(s, carry):
            fn(s)
            return carry
        lax.fori_loop(0, nblk, body, 0)

    @pl.when(nblk > 0)
    def _():
        @pl.when(j == 0)
        def _():
            for_blocks(lambda s: in_copy(s).start())
            for_blocks(lambda s: in_copy(s).wait())

        wgb[...] = wg_ref[0].astype(BF16)
        wub[...] = wu_ref[0].astype(BF16)
        wdb[...] = wd_ref[0].astype(BF16)

        @pl.when(j == 0)
        def _():
            def init(s):
                rows = pl.ds(pl.multiple_of(s * MOE_BLOCK, MOE_BLOCK), MOE_BLOCK)
                w = xbuf[rows, :]
                half = w.shape[1]
                xb16[rows, 0:half] = lax.bitcast_convert_type(w << 16, F32).astype(BF16)
                xb16[rows, half:] = lax.bitcast_convert_type(w & jnp.uint32(0xFFFF0000), F32).astype(BF16)
                acc[rows, :] = jnp.broadcast_to(bd_ref[0], (MOE_BLOCK, acc.shape[1]))

            for_blocks(init)

        def ffn(row_start, m):
            rows = pl.ds(pl.multiple_of(row_start, MOE_BLOCK), m)
            xb = xb16[rows, :]
            gate = _dot(xb, wgb[...]) + bg_ref[0]
            up = _dot(xb, wub[...]) + bu_ref[0]
            gate = jnp.minimum(gate, SWIGLU_LIMIT)
            up = jnp.clip(up, -SWIGLU_LIMIT, SWIGLU_LIMIT)
            act = (up + 1.0) * (gate * jax.nn.sigmoid(SWIGLU_ALPHA * gate))
            acc[rows, :] = acc[rows, :] + _dot(act.astype(BF16), wdb[...])

        n_span = nblk // FFN_SPAN

        def span_body(s, carry):
            ffn(s * (FFN_SPAN * MOE_BLOCK), FFN_SPAN * MOE_BLOCK)
            return carry

        lax.fori_loop(0, n_span, span_body, 0)

        done = n_span * FFN_SPAN
        pair = (nblk - done) >= 2

        @pl.when(pair)
        def _():
            ffn(done * MOE_BLOCK, 2 * MOE_BLOCK)

        def rest_body(s, carry):
            ffn(s * MOE_BLOCK, MOE_BLOCK)
            return carry

        lax.fori_loop(done + jnp.where(pair, 2, 0), nblk, rest_body, 0)

        @pl.when(j == pl.num_programs(1) - 1)
        def _():
            for_blocks(lambda s: out_copy(s).start())
            for_blocks(lambda s: out_copy(s).wait())

    @pl.when((g == pl.num_programs(0) - 1) & (j == pl.num_programs(1) - 1))
    def _():
        acc[0:MOE_BLOCK, :] = jnp.zeros((MOE_BLOCK, acc.shape[1]), acc.dtype)

        def tail_copy(s):
            off = pl.multiple_of(s * MOE_BLOCK, MOE_BLOCK)
            return pltpu.make_async_copy(acc.at[pl.ds(0, MOE_BLOCK)], ys_ref.at[pl.ds(off, MOE_BLOCK)], sem.at[1])

        def over_tail(fn):
            def body(s, carry):
                fn(s)
                return carry
            lax.fori_loop(gn_ref[pl.num_programs(0)], ys_ref.shape[0] // MOE_BLOCK, body, 0)

        over_tail(lambda s: tail_copy(s).start())
        over_tail(lambda s: tail_copy(s).wait())


def _experts(ge, gs, gn, xs, w_gu, b_gu, w_dn, b_dn):
    R = xs.shape[0]
    D = D_MODEL
    E = w_gu.shape[0]
    G = ge.shape[0]
    J = D_FF // FF_TILE

    def jj(g, j, gn_ref):
        return jnp.where(gn_ref[g] > 0, j, J - 1)

    return pl.pallas_call(
        _experts_kernel,
        grid_spec=pltpu.PrefetchScalarGridSpec(
            num_scalar_prefetch=3,
            grid=(G, J),
            in_specs=[pl.BlockSpec(memory_space=pl.ANY),
                      pl.BlockSpec((1, D, FF_TILE), lambda g, j, ge, gs, gn: (ge[g], 0, jj(g, j, gn))),
                      pl.BlockSpec((1, D, FF_TILE), lambda g, j, ge, gs, gn: (ge[g], 0, J + jj(g, j, gn))),
                      pl.BlockSpec((1, FF_TILE, D), lambda g, j, ge, gs, gn: (ge[g], jj(g, j, gn), 0)),
                      pl.BlockSpec((1, 1, FF_TILE), lambda g, j, ge, gs, gn: (ge[g], 0, jj(g, j, gn))),
                      pl.BlockSpec((1, 1, FF_TILE), lambda g, j, ge, gs, gn: (ge[g], 0, J + jj(g, j, gn))),
                      pl.BlockSpec((1, 1, D), lambda g, j, ge, gs, gn: (ge[g], 0, 0))],
            out_specs=pl.BlockSpec(memory_space=pl.ANY),
            scratch_shapes=[pltpu.VMEM((GROUP_ROWS, D // 2), jnp.uint32),
                            pltpu.VMEM((GROUP_ROWS, D), BF16),
                            pltpu.VMEM((GROUP_ROWS, D), F32),
                            pltpu.VMEM((D, FF_TILE), BF16),
                            pltpu.VMEM((D, FF_TILE), BF16),
                            pltpu.VMEM((FF_TILE, D), BF16),
                            pltpu.SemaphoreType.DMA((2,))]),
        out_shape=jax.ShapeDtypeStruct((R, D), F32),
        compiler_params=_cparams(("arbitrary", "arbitrary")),
        name="experts",
    )(ge, gs, gn, xs, w_gu, w_gu, w_dn, b_gu.reshape(E, 1, 2 * D_FF), b_gu.reshape(E, 1, 2 * D_FF),
      b_dn.reshape(E, 1, D))


def _combine_kernel(dest_ref, ys_ref, x1_ref, gate_ref, gt_ref, fg_ref, o_ref, ybuf, sem):
    i = pl.program_id(0)
    tm = x1_ref.shape[0]
    n_tok = pl.num_programs(0) * tm

    def copy(t, j):
        d = dest_ref[j * n_tok + i * tm + t]
        return pltpu.make_async_copy(ys_ref.at[pl.ds(d, 1)], ybuf.at[j, pl.ds(t, 1)], sem)

    def issue(t, carry):
        for j in range(TOP_K):
            copy(t, j).start()
        return carry

    lax.fori_loop(0, tm, issue, 0)

    def drain(t, carry):
        for j in range(TOP_K):
            copy(t, j).wait()
        return carry

    lax.fori_loop(0, tm, drain, 0)

    gts = gate_ref[...]
    y = ybuf[0] * gts[:, 0:1]
    for j in range(1, TOP_K):
        y = y + ybuf[j] * gts[:, j:j + 1]
    x2 = x1_ref[...] + gt_ref[0] * y
    o_ref[...] = (x2 * lax.rsqrt(jnp.mean(x2 * x2, axis=-1, keepdims=True) + RMS_EPS)) * fg_ref[...]


def _combine(dest_flat, ys, x1, gate_cols, gt, final_g, seq):
    N, D = x1.shape
    tm = 256
    tiles_per_b = seq // tm
    return pl.pallas_call(
        _combine_kernel,
        grid_spec=pltpu.PrefetchScalarGridSpec(
            num_scalar_prefetch=1,
            grid=(N // tm,),
            in_specs=[pl.BlockSpec(memory_space=pl.ANY),
                      pl.BlockSpec((tm, D), lambda i, d: (i, 0)),
                      pl.BlockSpec((tm, 128), lambda i, d: (i, 0)),
                      pl.BlockSpec((1, 1, D), lambda i, d: (i // tiles_per_b, 0, 0)),
                      pl.BlockSpec((1, D), lambda i, d: (0, 0))],
            out_specs=pl.BlockSpec((tm, D), lambda i, d: (i, 0)),
            scratch_shapes=[pltpu.VMEM((TOP_K, tm, D), F32),
                            pltpu.SemaphoreType.DMA(())]),
        out_shape=jax.ShapeDtypeStruct((N, D), F32),
        compiler_params=_cparams(("arbitrary",)),
        name="combine",
    )(dest_flat, ys, x1, gate_cols, gt, final_g.reshape(1, D))


_W_IN_SEGMENTS = (
    (0, 1024, COL_R), (1088, 1024, COL_K), (2112, 1024, COL_V),
    (3360, 1024, COL_Q), (4384, 1024, COL_MK), (5408, 1024, COL_MV), (6432, 1024, COL_MO),
    (3200, 160, COL_GL), (1024, 64, COL_LORA), (3136, 64, COL_LORA + 64),
    (7456, 8, COL_IF),
)


def _relayout_kernel(w_ref, o_ref):
    o_ref[...] = jnp.zeros(o_ref.shape, o_ref.dtype)
    for src, width, dst in _W_IN_SEGMENTS:
        o_ref[:, dst:dst + width] = w_ref[:, src:src + width].astype(o_ref.dtype)


def _relayout_w_in(w_in):
    D, n_in = w_in.shape
    tr = 256
    return pl.pallas_call(
        _relayout_kernel,
        grid=(D // tr,),
        in_specs=[pl.BlockSpec((tr, n_in), lambda i: (i, 0))],
        out_specs=pl.BlockSpec((tr, IN_COLS_P), lambda i: (i, 0)),
        out_shape=jax.ShapeDtypeStruct((D, IN_COLS_P), BF16),
        compiler_params=_cparams(("parallel",)),
        name="relayout_w_in",
    )(w_in)


def kernel(x, c, ada_w, ada_b, norm1_g, w_in, rwkv_mu, rwkv_w0, rwkv_w2, rwkv_a0, rwkv_a2, rwkv_g2, rwkv_kk, rwkv_ka, rwkv_rk, rwkv_ln_w, rwkv_ln_b, mlstm_conv_w, mlstm_conv_b, mlstm_b_i, mlstm_b_f, mlstm_norm_g, w_out, norm2_g, router_w, router_b, moe_w_gu, moe_b_gu, moe_w_dn, moe_b_dn, final_g):
    B, T, D = x.shape
    N = B * T
    x2 = x.reshape(N, D)
    l = 0

    mod = _adaln(c, ada_w[l], ada_b[l])
    sh_m, sc_m, gt_m, sh_f, sc_f, gt_f = [m.reshape(B, 1, D) for m in jnp.split(mod, 6, axis=-1)]

    proj = _in_proj(x2, norm1_g[l], sc_m, sh_m, _relayout_w_in(w_in[l]), T)

    mu = rwkv_mu[l]
    mu_r, mu_wl, mu_k, mu_v, mu_al, mu_gl = (mu[0:1024], mu[1024:1088], mu[1088:2112], mu[2112:3136],
                                             mu[3136:3200], mu[3200:3360])
    row = lambda a: a.reshape(1, -1)
    zrows = lambda n: jnp.zeros((n, RWKV_DIM), F32)
    rw = {
        "mu_r": row(mu_r), "mu_k": row(mu_k), "mu_v": row(mu_v),
        "mu_gl": row(jnp.concatenate([mu_gl, jnp.zeros((96,), F32)])),
        "mu_lo": row(jnp.concatenate([mu_wl, mu_al])),
        "w0": row(rwkv_w0[l]), "a0": row(rwkv_a0[l]), "kk": row(rwkv_kk[l]), "ka": row(rwkv_ka[l]),
        "rk": row(rwkv_rk[l]), "ln_w": row(rwkv_ln_w[l]), "ln_b": row(rwkv_ln_b[l]),
        "w2": jnp.concatenate([rwkv_w2[l], zrows(A_LORA)], axis=0),
        "a2": jnp.concatenate([zrows(DECAY_LORA), rwkv_a2[l]], axis=0),
        "g2": jnp.concatenate([rwkv_g2[l], zrows(256 - G_LORA)], axis=0).astype(BF16),
    }
    y_rwkv = _rwkv(proj, B, T, rw)

    cw = mlstm_conv_w[l]
    cb = mlstm_conv_b[l]
    ml = {
        "cw_q": cw[:, :MLSTM_DIM], "cw_k": cw[:, MLSTM_DIM:],
        "cb_q": row(cb[:MLSTM_DIM]), "cb_k": row(cb[MLSTM_DIM:]),
        "gbias": row(jnp.concatenate([mlstm_b_i[l], mlstm_b_f[l], jnp.zeros((120,), F32)])),
        "norm_g": row(mlstm_norm_g[l]),
    }
    y_mlstm = _mlstm(proj, B, T, ml)

    rw_p = jnp.concatenate([router_w[l], jnp.zeros((D, 128 - N_EXPERTS), F32)], axis=1)
    rb_p = jnp.concatenate([router_b[l], jnp.zeros((128 - N_EXPERTS,), F32)]).reshape(1, 128)
    x1, h2, idx, gate_cols, rank, cnt = _out_route(
        y_rwkv, y_mlstm, x2, w_out[l].astype(BF16), gt_m, norm2_g[l], sc_f, sh_f, rw_p, rb_p, T)

    dest = _dest(idx, rank, cnt).reshape(-1)

    counts = cnt[:, 0]
    per = GROUP_ROWS // MOE_BLOCK
    nblk = (counts + MOE_BLOCK - 1) // MOE_BLOCK
    blk_start = jnp.cumsum(nblk) - nblk
    ngrp = (nblk + per - 1) // per
    gend = jnp.cumsum(ngrp)
    gstart = gend - ngrp
    n_groups = N_EXPERTS + (N * TOP_K) // GROUP_ROWS
    gid = jnp.arange(n_groups, dtype=jnp.int32)
    ge_raw = jnp.minimum(jnp.searchsorted(gend, gid, side="right"), N_EXPERTS - 1).astype(jnp.int32)
    valid = gid < gend[-1]
    last_e = jnp.minimum(jnp.searchsorted(gend, jnp.maximum(gend[-1] - 1, 0), side="right"),
                         N_EXPERTS - 1).astype(jnp.int32)
    ge = jnp.where(valid, ge_raw, last_e).astype(jnp.int32)
    within = gid - gstart[ge_raw]
    gs = jnp.where(valid, blk_start[ge_raw] + within * per, 0).astype(jnp.int32)
    gn = jnp.where(valid, jnp.minimum(nblk[ge_raw] - within * per, per), 0).astype(jnp.int32)
    gn = jnp.concatenate([gn, jnp.sum(nblk, keepdims=True).astype(jnp.int32)])

    n_rows = ((N * TOP_K) // MOE_BLOCK + N_EXPERTS) * MOE_BLOCK
    xs = _dispatch(dest, h2, n_rows)
    ys = _experts(ge, gs, gn, xs, moe_w_gu[l], moe_b_gu[l], moe_w_dn[l], moe_b_dn[l])
    out = _combine(dest, ys, x1, gate_cols, gt_f, final_g, T)
    return out.reshape(B, T, D)
```

```python
import functools

import jax
import jax.numpy as jnp
from jax import lax
from jax.experimental import pallas as pl
from jax.experimental.pallas import tpu as pltpu

F32 = jnp.float32
BF16 = jnp.bfloat16
HIGHEST = lax.Precision.HIGHEST

D_MODEL = 2048
CHUNK = 64
RMS_EPS = 1e-5
RWKV_DIM = 1024
RWKV_HEAD = 64
DECAY_LORA = 64
A_LORA = 64
G_LORA = 160
GN_EPS = 64e-5
MLSTM_DIM = 1024
MLSTM_HEADS = 4
MLSTM_HEAD = 256
CONV_W = 4
HEAD_NORM_EPS = 1e-6
N_EXPERTS = 32
TOP_K = 4
D_FF = 2048
SWIGLU_LIMIT = 7.0
SWIGLU_ALPHA = 1.702
MOE_BLOCK = 256

COL_R, COL_K, COL_V, COL_Q, COL_MK, COL_MV, COL_MO = (i * 1024 for i in range(7))
COL_GL = 7168
COL_LORA = 7424
COL_IF = 7552
IN_COLS_P = 7680
QUAD = 256

VMEM_LIMIT = 56 * 1024 * 1024


def _cparams(sem, vmem=VMEM_LIMIT):
    return pltpu.CompilerParams(dimension_semantics=sem, vmem_limit_bytes=vmem)


def _dot(a, b, **kw):
    return jnp.dot(a, b, preferred_element_type=F32, **kw)


def _dot_nt(a, b, **kw):
    return lax.dot_general(a, b, (((1,), (1,)), ((), ())), preferred_element_type=F32, **kw)


def _dot_tn(a, b, **kw):
    return lax.dot_general(a, b, (((0,), (0,)), ((), ())), preferred_element_type=F32, **kw)


def _split3(x):
    h = x.astype(BF16)
    r1 = x - h.astype(F32)
    m = r1.astype(BF16)
    l = (r1 - m.astype(F32)).astype(BF16)
    return h, m, l


def _iota(shape, dim):
    return lax.broadcasted_iota(jnp.int32, shape, dim)


def _adaln_kernel(c_ref, w_ref, b_ref, o_ref):
    c = c_ref[...]
    s = c * jax.nn.sigmoid(c)
    o_ref[...] = _dot(s, w_ref[...], precision=HIGHEST) + b_ref[...]


def _adaln(c, ada_w, ada_b):
    B, D = c.shape
    n_out = ada_w.shape[1]
    tn = 1024
    c8 = jnp.zeros((8, D), F32).at[:B].set(c)
    out = pl.pallas_call(
        _adaln_kernel,
        grid=(n_out // tn,),
        in_specs=[pl.BlockSpec((8, D), lambda j: (0, 0)),
                  pl.BlockSpec((D, tn), lambda j: (0, j)),
                  pl.BlockSpec((1, tn), lambda j: (0, j))],
        out_specs=pl.BlockSpec((8, tn), lambda j: (0, j)),
        out_shape=jax.ShapeDtypeStruct((8, n_out), F32),
        compiler_params=_cparams(("parallel",)),
        name="adaln",
    )(c8, ada_w, ada_b.reshape(1, n_out))
    return out[:B]


def _inproj_kernel(x_ref, g_ref, sc_ref, sh_ref, w_ref, o_ref, h_scr):
    @pl.when(pl.program_id(1) == 0)
    def _():
        x = x_ref[...]
        y = x * lax.rsqrt(jnp.mean(x * x, axis=-1, keepdims=True) + RMS_EPS)
        h = (y * g_ref[...]) * (1.0 + sc_ref[0]) + sh_ref[0]
        h_scr[...] = h.astype(BF16)

    o_ref[...] = _dot(h_scr[...], w_ref[...])


def _in_proj(x2, norm_g, sc, sh, w_p, seq):
    N, D = x2.shape
    tm, tn = 1024, 1536
    tiles_per_b = seq // tm
    ncols = w_p.shape[1]
    return pl.pallas_call(
        _inproj_kernel,
        grid=(N // tm, ncols // tn),
        in_specs=[pl.BlockSpec((tm, D), lambda i, j: (i, 0)),
                  pl.BlockSpec((1, D), lambda i, j: (0, 0)),
                  pl.BlockSpec((1, 1, D), lambda i, j: (i // tiles_per_b, 0, 0)),
                  pl.BlockSpec((1, 1, D), lambda i, j: (i // tiles_per_b, 0, 0)),
                  pl.BlockSpec((D, tn), lambda i, j: (0, j))],
        out_specs=pl.BlockSpec((tm, tn), lambda i, j: (i, j)),
        out_shape=jax.ShapeDtypeStruct((N, ncols), F32),
        scratch_shapes=[pltpu.VMEM((tm, D), BF16)],
        compiler_params=_cparams(("parallel", "arbitrary")),
        name="in_proj",
    )(x2, norm_g.reshape(1, D), sc, sh, w_p)


def _shift_rows(x, prev_row):
    rolled = pltpu.roll(x, 1, axis=0)
    row = _iota(x.shape, 0)
    return jnp.where(row == 0, prev_row, rolled)


def _rwkv_kernel(r_ref, k_ref, v_ref, gl_ref, lo_ref,
                 mu_r_ref, mu_k_ref, mu_v_ref, mu_gl_ref, mu_lo_ref,
                 w0_ref, a0_ref, kkp_ref, kap_ref, rk_ref, lnw_ref, lnb_ref,
                 w2_ref, a2_ref, g2_ref,
                 o_ref,
                 s_scr, prev_scr):
    L = CHUNK
    c = pl.program_id(1)

    @pl.when(c == 0)
    def _():
        s_scr[...] = jnp.zeros_like(s_scr)
        prev_scr[...] = jnp.zeros_like(prev_scr)

    def lerp(ref, mu_ref, lo, width):
        x = ref[...]
        prev = prev_scr[0:1, lo:lo + width]
        out = x + (_shift_rows(x, prev) - x) * mu_ref[...]
        prev_scr[0:1, lo:lo + width] = x[L - 1:L, :]
        return out

    r = lerp(r_ref, mu_r_ref, 0, RWKV_DIM)
    k = lerp(k_ref, mu_k_ref, 1024, RWKV_DIM)
    v = lerp(v_ref, mu_v_ref, 2048, RWKV_DIM)
    gl = lerp(gl_ref, mu_gl_ref, 3072, 256)
    lo = lerp(lo_ref, mu_lo_ref, 3328, 128)

    w = -jax.nn.softplus(-(w0_ref[...] + _dot(jnp.tanh(lo), w2_ref[...], precision=HIGHEST))) - 0.5
    lw = -jnp.exp(w)
    a = jax.nn.sigmoid(a0_ref[...] + _dot(lo, a2_ref[...], precision=HIGHEST))
    g = _dot(jax.nn.sigmoid(gl).astype(BF16), g2_ref[...])
    kkraw = k * kkp_ref[...]
    k2 = k * (1.0 + (a - 1.0) * kap_ref[...])

    ltri = jnp.where(_iota((L, L), 1) <= _iota((L, L), 0), 1.0, 0.0).astype(BF16)
    qr = _iota((QUAD, QUAD), 0)
    qc = _iota((QUAD, QUAD), 1)
    bd_mask = (qr // RWKV_HEAD) == (qc // RWKV_HEAD)
    bd = jnp.where(bd_mask, 1.0, 0.0).astype(BF16)
    strict_bd = bd_mask & ((qc % L) < (qr % L))
    incl_bd = bd_mask & ((qc % L) <= (qr % L))
    lane_head = _iota((1, QUAD), 1) // RWKV_HEAD
    n_heads = QUAD // RWKV_HEAD
    quads = range(RWKV_DIM // QUAD)

    def head_sum(x):
        h, m, l = _split3(x)
        return _dot(h, bd) + _dot(m, bd) + _dot(l, bd)

    def time_cumsum(x):
        h, m, l = _split3(x)
        return _dot(ltri, h) + _dot(ltri, m) + _dot(ltri, l)

    def stack_heads(x):
        return jnp.concatenate([jnp.where(lane_head == h, x, jnp.zeros_like(x)) for h in range(n_heads)], axis=0)

    def unstack_heads(x):
        out = x[0:L]
        for h in range(1, n_heads):
            out = out + x[h * L:(h + 1) * L]
        return out

    sls = [slice(q * QUAD, (q + 1) * QUAD) for q in quads]

    kk = []
    for sl in sls:
        kkr = kkraw[:, sl]
        kk.append(kkr / jnp.maximum(jnp.sqrt(head_sum(kkr * kkr)), 1e-12))
    cl = [time_cumsum(lw[:, sl]) for sl in sls]

    at, rt, bt, kt, p_last = [], [], [], [], []
    for q, sl in enumerate(sls):
        p_t = jnp.exp(cl[q])
        p_prev = jnp.exp(cl[q] - lw[:, sl])
        p_inv = jnp.exp(-cl[q])
        at.append((-kk[q] * p_prev).astype(BF16))
        rt.append((r[:, sl] * p_t).astype(BF16))
        bt.append((kk[q] * a[:, sl] * p_inv).astype(BF16))
        kt.append((k2[:, sl] * p_inv).astype(BF16))
        p_last.append(p_t[L - 1:L, :])

    s0 = [s_scr[q] for q in quads]
    n_pow, a_ak, a_rb, a_rk, rb4, rk4, vst, xs_st, y_state = [], [], [], [], [], [], [], [], []
    for q, sl in enumerate(sls):
        lst = jnp.concatenate([stack_heads(at[q]), stack_heads(rt[q])], axis=0)
        rb4.append(jnp.concatenate([bt[q]] * n_heads, axis=0))
        rk4.append(jnp.concatenate([kt[q]] * n_heads, axis=0))
        gb = _dot_nt(lst, rb4[q])
        gk = _dot_nt(lst, rk4[q])
        n_pow.append(jnp.where(strict_bd, gb[:QUAD], 0.0))
        a_rb.append(jnp.where(incl_bd, gb[QUAD:], 0.0).astype(BF16))
        a_ak.append(jnp.where(strict_bd, gk[:QUAD], 0.0).astype(BF16))
        a_rk.append(jnp.where(incl_bd, gk[QUAD:], 0.0).astype(BF16))
        vst.append(stack_heads(v[:, sl].astype(BF16)))
        xy = _dot_nt(jnp.concatenate([at[q], rt[q]], axis=0), s0[q].astype(BF16))
        xs_st.append(stack_heads(xy[:L]))
        y_state.append(xy[L:])

    u = [xs_st[q] + _dot(a_ak[q], vst[q]) for q in quads]
    for j in range(6):
        nb = [n_pow[q].astype(BF16) for q in quads]
        u = [u[q] + _dot(nb[q], u[q].astype(BF16)) for q in quads]
        if j < 5:
            n_pow = [_dot(nb[q], nb[q]) for q in quads]

    ub = [u[q].astype(BF16) for q in quads]
    y_st = [_dot(a_rb[q], ub[q]) + _dot(a_rk[q], vst[q]) for q in quads]
    for q in quads:
        z = s0[q] + _dot_tn(ub[q], rb4[q]) + _dot_tn(vst[q], rk4[q])
        s_scr[q] = jnp.where(bd_mask, z, 0.0) * p_last[q]

    inv_n = 1.0 / RWKV_HEAD
    y = [y_state[q] + unstack_heads(y_st[q]) for q in quads]
    mean = [head_sum(y[q]) * inv_n for q in quads]
    d = [y[q] - mean[q] for q in quads]
    var = [head_sum(d[q] * d[q]) * inv_n for q in quads]
    bonus = [head_sum(r[:, sl] * k2[:, sl] * rk_ref[:, sl]) * v[:, sl] for sl in sls]
    for q, sl in enumerate(sls):
        yn = d[q] * lax.rsqrt(var[q] + GN_EPS) * lnw_ref[:, sl] + lnb_ref[:, sl]
        o_ref[:, sl] = ((yn + bonus[q]) * g[:, sl]).astype(o_ref.dtype)


def _rwkv(proj, B, T, pr):
    NC = T // CHUNK
    L = CHUNK
    row = lambda b, c: b * NC + c

    def pspec(width, colblk):
        return pl.BlockSpec((L, width), lambda b, c: (row(b, c), colblk))

    def vec(width):
        return pl.BlockSpec((1, width), lambda b, c: (0, 0))

    def mat(rows):
        return pl.BlockSpec((rows, RWKV_DIM), lambda b, c: (0, 0))

    return pl.pallas_call(
        _rwkv_kernel,
        grid=(B, NC),
        in_specs=[pspec(1024, COL_R // 1024), pspec(1024, COL_K // 1024), pspec(1024, COL_V // 1024),
                  pspec(256, COL_GL // 256), pspec(128, COL_LORA // 128),
                  vec(1024), vec(1024), vec(1024), vec(256), vec(128),
                  vec(1024), vec(1024), vec(1024), vec(1024), vec(1024), vec(1024), vec(1024),
                  mat(128), mat(128), mat(256)],
        out_specs=pl.BlockSpec((L, RWKV_DIM), lambda b, c: (row(b, c), 0)),
        out_shape=jax.ShapeDtypeStruct((B * T, RWKV_DIM), BF16),
        scratch_shapes=[pltpu.VMEM((RWKV_DIM // QUAD, QUAD, QUAD), F32),
                        pltpu.VMEM((8, 3072 + 256 + 128), F32)],
        compiler_params=_cparams(("parallel", "arbitrary")),
        name="rwkv",
    )(proj, proj, proj, proj, proj,
      pr["mu_r"], pr["mu_k"], pr["mu_v"], pr["mu_gl"], pr["mu_lo"],
      pr["w0"], pr["a0"], pr["kk"], pr["ka"], pr["rk"], pr["ln_w"], pr["ln_b"],
      pr["w2"], pr["a2"], pr["g2"])


def _mlstm_kernel(q_ref, k_ref, v_ref, o_ref, gt_ref,
                  cwq_ref, cwk_ref, cbq_ref, cbk_ref, gbias_ref, ng_ref,
                  out_ref,
                  c_scr, n_scr, m_scr, prevq_scr, prevk_scr):
    L = CHUNK
    c = pl.program_id(1)

    @pl.when(c == 0)
    def _():
        c_scr[...] = jnp.zeros_like(c_scr)
        n_scr[...] = jnp.zeros_like(n_scr)
        m_scr[...] = jnp.zeros_like(m_scr)
        prevq_scr[...] = jnp.zeros_like(prevq_scr)
        prevk_scr[...] = jnp.zeros_like(prevk_scr)

    def conv_silu(ref, prev_scr, cw_ref, cb_ref):
        x = ref[...]
        prev8 = prev_scr[...]
        row8 = _iota(prev8.shape, 0)
        out = cb_ref[...] + x * cw_ref[CONV_W - 1:CONV_W, :]
        for d in range(1, CONV_W):
            rolled = pltpu.roll(x, d, axis=0)
            first8 = jnp.where(row8 < d, pltpu.roll(prev8, d, axis=0), rolled[:8])
            xs = jnp.concatenate([first8, rolled[8:]], axis=0)
            out = out + xs * cw_ref[CONV_W - 1 - d:CONV_W - d, :]
        prev_scr[...] = x[L - 8:L, :]
        return out * jax.nn.sigmoid(out)

    qf = conv_silu(q_ref, prevq_scr, cwq_ref, cbq_ref)
    kf = conv_silu(k_ref, prevk_scr, cwk_ref, cbk_ref) * (MLSTM_HEAD ** -0.5)
    vf = v_ref[...]

    z = gt_ref[...] + gbias_ref[...]
    lane = _iota(z.shape, 1)
    gc = jnp.where(lane < MLSTM_HEADS, z,
                   jnp.where(lane < 2 * MLSTM_HEADS, jax.nn.log_sigmoid(z), 0.0))
    row = _iota((L, L), 0)
    col = _iota((L, L), 1)
    causal = col <= row
    ltri = jnp.where(causal, 1.0, 0.0).astype(BF16)
    gh, gm, glo = _split3(gc)
    cum = _dot(ltri, gh) + _dot(ltri, gm) + _dot(ltri, glo)
    gc_t = gc.T
    cum_t = cum.T

    for h in range(MLSTM_HEADS):
        sl = slice(h * MLSTM_HEAD, (h + 1) * MLSTM_HEAD)
        qh, kh, vh = qf[:, sl], kf[:, sl], vf[:, sl]
        li_row = gc_t[h:h + 1, :]
        li_col = gc[:, h:h + 1]
        b_row = cum_t[MLSTM_HEADS + h:MLSTM_HEADS + h + 1, :]
        b_col = cum[:, MLSTM_HEADS + h:MLSTM_HEADS + h + 1]
        m_prev = m_scr[h:h + 1, 0:1]
        c_prev = c_scr[h]
        n_prev = n_scr[h:h + 1, :]

        a_inter = b_col + m_prev
        dm = jnp.where(causal, b_col - b_row + li_row, -jnp.inf)
        m_t = jnp.maximum(a_inter, jnp.max(dm, axis=-1, keepdims=True))
        w_inter = jnp.exp(a_inter - m_t)
        wmat = jnp.exp(dm - m_t)
        qb, kb, vb = qh.astype(BF16), kh.astype(BF16), vh.astype(BF16)
        s = _dot_nt(qb, kb) * wmat
        num = w_inter * _dot_nt(qb, c_prev.astype(BF16)) + _dot(s.astype(BF16), vb)
        den = w_inter * jnp.sum(qh * n_prev, axis=-1, keepdims=True) + jnp.sum(s, axis=-1, keepdims=True)
        hh = num / jnp.maximum(jnp.abs(den), jnp.exp(-m_t))

        m_new = m_t[L - 1:L, :]
        b_last = b_col[L - 1:L, :]
        g_state = jnp.exp(b_last + m_prev - m_new)
        w_s = jnp.exp(b_last - b_col + li_col - m_new)
        c_scr[h] = g_state * c_prev + _dot_tn((vh * w_s).astype(BF16), kb)
        n_scr[h:h + 1, :] = g_state * n_prev + jnp.sum(kh * w_s, axis=0, keepdims=True)
        m_scr[h:h + 1, :] = jnp.broadcast_to(m_new, (1, m_scr.shape[1]))

        hn = hh * lax.rsqrt(jnp.mean(hh * hh, axis=-1, keepdims=True) + HEAD_NORM_EPS)
        out_ref[:, sl] = ((hn * ng_ref[:, sl]) * jax.nn.sigmoid(o_ref[:, sl])).astype(out_ref.dtype)


def _mlstm(proj, B, T, pr):
    NC = T // CHUNK
    L = CHUNK
    row = lambda b, c: b * NC + c

    def pspec(width, colblk):
        return pl.BlockSpec((L, width), lambda b, c: (row(b, c), colblk))

    def cst(shape):
        return pl.BlockSpec(shape, lambda b, c: (0, 0))

    return pl.pallas_call(
        _mlstm_kernel,
        grid=(B, NC),
        in_specs=[pspec(1024, COL_Q // 1024), pspec(1024, COL_MK // 1024), pspec(1024, COL_MV // 1024),
                  pspec(1024, COL_MO // 1024), pspec(128, COL_IF // 128),
                  cst((CONV_W, 1024)), cst((CONV_W, 1024)), cst((1, 1024)), cst((1, 1024)),
                  cst((1, 128)), cst((1, 1024))],
        out_specs=pl.BlockSpec((L, MLSTM_DIM), lambda b, c: (row(b, c), 0)),
        out_shape=jax.ShapeDtypeStruct((B * T, MLSTM_DIM), BF16),
        scratch_shapes=[pltpu.VMEM((MLSTM_HEADS, MLSTM_HEAD, MLSTM_HEAD), F32),
                        pltpu.VMEM((8, MLSTM_HEAD), F32),
                        pltpu.VMEM((8, 128), F32),
                        pltpu.VMEM((8, 1024), F32),
                        pltpu.VMEM((8, 1024), F32)],
        compiler_params=_cparams(("parallel", "arbitrary")),
        name="mlstm",
    )(proj, proj, proj, proj, proj,
      pr["cw_q"], pr["cw_k"], pr["cb_q"], pr["cb_k"], pr["gbias"], pr["norm_g"])


def _out_route_kernel(yr_ref, ym_ref, x_ref, wo_ref, gt_ref, g2_ref, sc_ref, sh_ref, rw_ref, rb_ref,
                      x1_ref, h2_ref, idx_ref, gate_ref, rank_ref, cnt_ref,
                      carry_scr):
    i = pl.program_id(0)
    tm = x_ref.shape[0]

    @pl.when(i == 0)
    def _():
        carry_scr[...] = jnp.zeros_like(carry_scr)

    mix = _dot(yr_ref[...], wo_ref[0:RWKV_DIM, :]) + _dot(ym_ref[...], wo_ref[RWKV_DIM:, :])
    x1 = x_ref[...] + gt_ref[0] * mix
    x1_ref[...] = x1
    y = x1 * lax.rsqrt(jnp.mean(x1 * x1, axis=-1, keepdims=True) + RMS_EPS)
    h2 = (y * g2_ref[...]) * (1.0 + sc_ref[0]) + sh_ref[0]
    hbits = lax.bitcast_convert_type(h2.astype(BF16).astype(F32), jnp.uint32)
    half = h2.shape[1] // 2
    h2_ref[...] = (hbits[:, :half] >> 16) | (hbits[:, half:] & jnp.uint32(0xFFFF0000))

    logits = _dot(h2, rw_ref[...], precision=HIGHEST) + rb_ref[...]
    lt = logits.T[:N_EXPERTS, :]
    e_iota = _iota(lt.shape, 0)
    onehots, vals, idxs = [], [], []
    for _ in range(TOP_K):
        mx = jnp.max(lt, axis=0, keepdims=True)
        idx = jnp.min(jnp.where(lt == mx, e_iota, N_EXPERTS), axis=0, keepdims=True)
        sel = e_iota == idx
        onehots.append(sel)
        vals.append(mx)
        idxs.append(idx)
        lt = jnp.where(sel, -jnp.inf, lt)
    exps = [jnp.exp(vv - vals[0]) for vv in vals]
    denom = exps[0] + exps[1] + exps[2] + exps[3]
    gates = [e / denom for e in exps]

    member = jnp.zeros(onehots[0].shape, F32)
    for sel in onehots:
        member = member + jnp.where(sel, 1.0, 0.0)
    ur = _iota((tm, tm), 0)
    uc = _iota((tm, tm), 1)
    ustrict = jnp.where(ur < uc, 1.0, 0.0).astype(BF16)
    before = _dot(member.astype(BF16), ustrict) + carry_scr[:, 0:1]
    for j in range(TOP_K):
        rank = jnp.sum(jnp.where(onehots[j], before, 0.0), axis=0, keepdims=True)
        rank_ref[j:j + 1, :] = rank.astype(jnp.int32)
        idx_ref[j:j + 1, :] = idxs[j]
    new_carry = carry_scr[...] + jnp.sum(member, axis=1, keepdims=True)
    carry_scr[...] = new_carry
    cnt_ref[...] = new_carry.astype(jnp.int32)

    grows = jnp.concatenate(gates + [jnp.zeros((128 - TOP_K, tm), F32)], axis=0)
    gate_ref[...] = grows.T


def _out_route(yr, ym, x2, w_out_b, gt, g2, sc, sh, rw_p, rb_p, seq):
    N, D = x2.shape
    tm = 512
    tiles_per_b = seq // tm
    bvec = pl.BlockSpec((1, 1, D), lambda i: (i // tiles_per_b, 0, 0))
    return pl.pallas_call(
        _out_route_kernel,
        grid=(N // tm,),
        in_specs=[pl.BlockSpec((tm, RWKV_DIM), lambda i: (i, 0)),
                  pl.BlockSpec((tm, MLSTM_DIM), lambda i: (i, 0)),
                  pl.BlockSpec((tm, D), lambda i: (i, 0)),
                  pl.BlockSpec((D, D), lambda i: (0, 0)),
                  bvec,
                  pl.BlockSpec((1, D), lambda i: (0, 0)),
                  bvec, bvec,
                  pl.BlockSpec((D, 128), lambda i: (0, 0)),
                  pl.BlockSpec((1, 128), lambda i: (0, 0))],
        out_specs=[pl.BlockSpec((tm, D), lambda i: (i, 0)),
                   pl.BlockSpec((tm, D // 2), lambda i: (i, 0)),
                   pl.BlockSpec((TOP_K, tm), lambda i: (0, i)),
                   pl.BlockSpec((tm, 128), lambda i: (i, 0)),
                   pl.BlockSpec((TOP_K, tm), lambda i: (0, i)),
                   pl.BlockSpec((N_EXPERTS, 128), lambda i: (0, 0))],
        out_shape=[jax.ShapeDtypeStruct((N, D), F32),
                   jax.ShapeDtypeStruct((N, D // 2), jnp.uint32),
                   jax.ShapeDtypeStruct((TOP_K, N), jnp.int32),
                   jax.ShapeDtypeStruct((N, 128), F32),
                   jax.ShapeDtypeStruct((TOP_K, N), jnp.int32),
                   jax.ShapeDtypeStruct((N_EXPERTS, 128), jnp.int32)],
        scratch_shapes=[pltpu.VMEM((N_EXPERTS, 128), F32)],
        compiler_params=_cparams(("arbitrary",)),
        name="out_route",
    )(yr, ym, x2, w_out_b, gt, g2.reshape(1, D), sc, sh, rw_p, rb_p)


def _dest_kernel(idx_ref, rank_ref, cnt_ref, dest_ref):
    cnt = cnt_ref[:, 0:1].astype(F32)
    padded = jnp.floor((cnt + (MOE_BLOCK - 1)) * (1.0 / MOE_BLOCK)) * MOE_BLOCK
    er = _iota((N_EXPERTS, N_EXPERTS), 0)
    ec = _iota((N_EXPERTS, N_EXPERTS), 1)
    before = jnp.where(ec < er, 1.0, 0.0)
    pstart = _dot(before, jnp.broadcast_to(padded, (N_EXPERTS, 128)), precision=HIGHEST)[:, 0:1]
    e_iota = _iota((N_EXPERTS, idx_ref.shape[1]), 0)
    for j in range(TOP_K):
        sel = e_iota == idx_ref[j:j + 1, :]
        base = jnp.sum(jnp.where(sel, pstart, 0.0), axis=0, keepdims=True)
        dest_ref[j:j + 1, :] = base.astype(jnp.int32) + rank_ref[j:j + 1, :]


def _dest(idx, rank, cnt):
    N = idx.shape[1]
    tn = 2048
    return pl.pallas_call(
        _dest_kernel,
        grid=(N // tn,),
        in_specs=[pl.BlockSpec((TOP_K, tn), lambda i: (0, i)),
                  pl.BlockSpec((TOP_K, tn), lambda i: (0, i)),
                  pl.BlockSpec((N_EXPERTS, 128), lambda i: (0, 0))],
        out_specs=pl.BlockSpec((TOP_K, tn), lambda i: (0, i)),
        out_shape=jax.ShapeDtypeStruct((TOP_K, N), jnp.int32),
        compiler_params=_cparams(("parallel",)),
        name="dest",
    )(idx, rank, cnt)


def _dispatch_kernel(dest_ref, h_ref, xs_in_ref, xs_ref, sem):
    del xs_in_ref
    i = pl.program_id(0)
    tm = h_ref.shape[0]
    n_tok = pl.num_programs(0) * tm

    def copy(t, j):
        d = dest_ref[j * n_tok + i * tm + t]
        return pltpu.make_async_copy(h_ref.at[pl.ds(t, 1)], xs_ref.at[pl.ds(d, 1)], sem)

    def issue(t, carry):
        for j in range(TOP_K):
            copy(t, j).start()
        return carry

    lax.fori_loop(0, tm, issue, 0)

    def drain(t, carry):
        for j in range(TOP_K):
            copy(t, j).wait()
        return carry

    lax.fori_loop(0, tm, drain, 0)


def _dispatch(dest_flat, h2, n_rows):
    N, D = h2.shape
    tm = 256
    xs0 = jnp.zeros((n_rows, D), h2.dtype)
    return pl.pallas_call(
        _dispatch_kernel,
        grid_spec=pltpu.PrefetchScalarGridSpec(
            num_scalar_prefetch=1,
            grid=(N // tm,),
            in_specs=[pl.BlockSpec((tm, D), lambda i, d: (i, 0)),
                      pl.BlockSpec(memory_space=pl.ANY)],
            out_specs=pl.BlockSpec(memory_space=pl.ANY),
            scratch_shapes=[pltpu.SemaphoreType.DMA(())]),
        out_shape=jax.ShapeDtypeStruct((n_rows, D), h2.dtype),
        input_output_aliases={2: 0},
        compiler_params=_cparams(("arbitrary",)),
        name="dispatch",
    )(dest_flat, h2, xs0)


GROUP_ROWS = 2048
FF_TILE = 256
FFN_SPAN = 4


def _experts_kernel(ge_ref, gs_ref, gn_ref,
                    xs_ref, wg_ref, wu_ref, wd_ref, bg_ref, bu_ref, bd_ref,
                    ys_ref,
                    xbuf, xb16, acc, wgb, wub, wdb, sem):
    g = pl.program_id(0)
    j = pl.program_id(1)
    nblk = gn_ref[g]
    row0 = pl.multiple_of(gs_ref[g] * MOE_BLOCK, MOE_BLOCK)

    def in_copy(s):
        off = pl.multiple_of(s * MOE_BLOCK, MOE_BLOCK)
        return pltpu.make_async_copy(xs_ref.at[pl.ds(row0 + off, MOE_BLOCK)],
                                     xbuf.at[pl.ds(off, MOE_BLOCK)], sem.at[0])

    def out_copy(s):
        off = pl.multiple_of(s * MOE_BLOCK, MOE_BLOCK)
        return pltpu.make_async_copy(acc.at[pl.ds(off, MOE_BLOCK)],
                                     ys_ref.at[pl.ds(row0 + off, MOE_BLOCK)], sem.at[1])

    def for_blocks(fn):
        def body(s, carry):
            fn(s)
            return carry
        lax.fori_loop(0, nblk, body, 0)

    @pl.when(nblk > 0)
    def _():
        @pl.when(j == 0)
        def _():
            for_blocks(lambda s: in_copy(s).start())
            for_blocks(lambda s: in_copy(s).wait())

        wgb[...] = wg_ref[0].astype(BF16)
        wub[...] = wu_ref[0].astype(BF16)
        wdb[...] = wd_ref[0].astype(BF16)

        @pl.when(j == 0)
        def _():
            def init(s):
                rows = pl.ds(pl.multiple_of(s * MOE_BLOCK, MOE_BLOCK), MOE_BLOCK)
                w = xbuf[rows, :]
                half = w.shape[1]
                xb16[rows, 0:half] = lax.bitcast_convert_type(w << 16, F32).astype(BF16)
                xb16[rows, half:] = lax.bitcast_convert_type(w & jnp.uint32(0xFFFF0000), F32).astype(BF16)
                acc[rows, :] = jnp.broadcast_to(bd_ref[0], (MOE_BLOCK, acc.shape[1]))

            for_blocks(init)

        def ffn(row_start, m):
            rows = pl.ds(pl.multiple_of(row_start, MOE_BLOCK), m)
            xb = xb16[rows, :]
            gate = _dot(xb, wgb[...]) + bg_ref[0]
            up = _dot(xb, wub[...]) + bu_ref[0]
            gate = jnp.minimum(gate, SWIGLU_LIMIT)
            up = jnp.clip(up, -SWIGLU_LIMIT, SWIGLU_LIMIT)
            act = (up + 1.0) * (gate * jax.nn.sigmoid(SWIGLU_ALPHA * gate))
            acc[rows, :] = acc[rows, :] + _dot(act.astype(BF16), wdb[...])

        n_span = nblk // FFN_SPAN

        def span_body(s, carry):
            ffn(s * (FFN_SPAN * MOE_BLOCK), FFN_SPAN * MOE_BLOCK)
            return carry

        lax.fori_loop(0, n_span, span_body, 0)

        done = n_span * FFN_SPAN
        pair = (nblk - done) >= 2

        @pl.when(pair)
        def _():
            ffn(done * MOE_BLOCK, 2 * MOE_BLOCK)

        def rest_body(s, carry):
            ffn(s * MOE_BLOCK, MOE_BLOCK)
            return carry

        lax.fori_loop(done + jnp.where(pair, 2, 0), nblk, rest_body, 0)

        @pl.when(j == pl.num_programs(1) - 1)
        def _():
            for_blocks(lambda s: out_copy(s).start())
            for_blocks(lambda s: out_copy(s).wait())

    @pl.when((g == pl.num_programs(0) - 1) & (j == pl.num_programs(1) - 1))
    def _():
        acc[0:MOE_BLOCK, :] = jnp.zeros((MOE_BLOCK, acc.shape[1]), acc.dtype)

        def tail_copy(s):
            off = pl.multiple_of(s * MOE_BLOCK, MOE_BLOCK)
            return pltpu.make_async_copy(acc.at[pl.ds(0, MOE_BLOCK)], ys_ref.at[pl.ds(off, MOE_BLOCK)], sem.at[1])

        def over_tail(fn):
            def body(s, carry):
                fn(s)
                return carry
            lax.fori_loop(gn_ref[pl.num_programs(0)], ys_ref.shape[0] // MOE_BLOCK, body, 0)

        over_tail(lambda s: tail_copy(s).start())
        over_tail(lambda s: tail_copy(s).wait())


def _experts(ge, gs, gn, xs, w_gu, b_gu, w_dn, b_dn):
    R = xs.shape[0]
    D = D_MODEL
    E = w_gu.shape[0]
    G = ge.shape[0]
    J = D_FF // FF_TILE

    def jj(g, j, gn_ref):
        return jnp.where(gn_ref[g] > 0, j, J - 1)

    return pl.pallas_call(
        _experts_kernel,
        grid_spec=pltpu.PrefetchScalarGridSpec(
            num_scalar_prefetch=3,
            grid=(G, J),
            in_specs=[pl.BlockSpec(memory_space=pl.ANY),
                      pl.BlockSpec((1, D, FF_TILE), lambda g, j, ge, gs, gn: (ge[g], 0, jj(g, j, gn))),
                      pl.BlockSpec((1, D, FF_TILE), lambda g, j, ge, gs, gn: (ge[g], 0, J + jj(g, j, gn))),
                      pl.BlockSpec((1, FF_TILE, D), lambda g, j, ge, gs, gn: (ge[g], jj(g, j, gn), 0)),
                      pl.BlockSpec((1, 1, FF_TILE), lambda g, j, ge, gs, gn: (ge[g], 0, jj(g, j, gn))),
                      pl.BlockSpec((1, 1, FF_TILE), lambda g, j, ge, gs, gn: (ge[g], 0, J + jj(g, j, gn))),
                      pl.BlockSpec((1, 1, D), lambda g, j, ge, gs, gn: (ge[g], 0, 0))],
            out_specs=pl.BlockSpec(memory_space=pl.ANY),
            scratch_shapes=[pltpu.VMEM((GROUP_ROWS, D // 2), jnp.uint32),
                            pltpu.VMEM((GROUP_ROWS, D), BF16),
                            pltpu.VMEM((GROUP_ROWS, D), F32),
                            pltpu.VMEM((D, FF_TILE), BF16),
                            pltpu.VMEM((D, FF_TILE), BF16),
                            pltpu.VMEM((FF_TILE, D), BF16),
                            pltpu.SemaphoreType.DMA((2,))]),
        out_shape=jax.ShapeDtypeStruct((R, D), F32),
        compiler_params=_cparams(("arbitrary", "arbitrary")),
        name="experts",
    )(ge, gs, gn, xs, w_gu, w_gu, w_dn, b_gu.reshape(E, 1, 2 * D_FF), b_gu.reshape(E, 1, 2 * D_FF),
      b_dn.reshape(E, 1, D))


def _combine_kernel(dest_ref, ys_ref, x1_ref, gate_ref, gt_ref, fg_ref, o_ref, ybuf, sem):
    i = pl.program_id(0)
    tm = x1_ref.shape[0]
    n_tok = pl.num_programs(0) * tm

    def copy(t, j):
        d = dest_ref[j * n_tok + i * tm + t]
        return pltpu.make_async_copy(ys_ref.at[pl.ds(d, 1)], ybuf.at[j, pl.ds(t, 1)], sem)

    def issue(t, carry):
        for j in range(TOP_K):
            copy(t, j).start()
        return carry

    lax.fori_loop(0, tm, issue, 0)

    def drain(t, carry):
        for j in range(TOP_K):
            copy(t, j).wait()
        return carry

    lax.fori_loop(0, tm, drain, 0)

    gts = gate_ref[...]
    y = ybuf[0] * gts[:, 0:1]
    for j in range(1, TOP_K):
        y = y + ybuf[j] * gts[:, j:j + 1]
    x2 = x1_ref[...] + gt_ref[0] * y
    o_ref[...] = (x2 * lax.rsqrt(jnp.mean(x2 * x2, axis=-1, keepdims=True) + RMS_EPS)) * fg_ref[...]


def _combine(dest_flat, ys, x1, gate_cols, gt, final_g, seq):
    N, D = x1.shape
    tm = 256
    tiles_per_b = seq // tm
    return pl.pallas_call(
        _combine_kernel,
        grid_spec=pltpu.PrefetchScalarGridSpec(
            num_scalar_prefetch=1,
            grid=(N // tm,),
            in_specs=[pl.BlockSpec(memory_space=pl.ANY),
                      pl.BlockSpec((tm, D), lambda i, d: (i, 0)),
                      pl.BlockSpec((tm, 128), lambda i, d: (i, 0)),
                      pl.BlockSpec((1, 1, D), lambda i, d: (i // tiles_per_b, 0, 0)),
                      pl.BlockSpec((1, D), lambda i, d: (0, 0))],
            out_specs=pl.BlockSpec((tm, D), lambda i, d: (i, 0)),
            scratch_shapes=[pltpu.VMEM((TOP_K, tm, D), F32),
                            pltpu.SemaphoreType.DMA(())]),
        out_shape=jax.ShapeDtypeStruct((N, D), F32),
        compiler_params=_cparams(("arbitrary",)),
        name="combine",
    )(dest_flat, ys, x1, gate_cols, gt, final_g.reshape(1, D))


_W_IN_SEGMENTS = (
    (0, 1024, COL_R), (1088, 1024, COL_K), (2112, 1024, COL_V),
    (3360, 1024, COL_Q), (4384, 1024, COL_MK), (5408, 1024, COL_MV), (6432, 1024, COL_MO),
    (3200, 160, COL_GL), (1024, 64, COL_LORA), (3136, 64, COL_LORA + 64),
    (7456, 8, COL_IF),
)


def _relayout_kernel(w_ref, o_ref):
    o_ref[...] = jnp.zeros(o_ref.shape, o_ref.dtype)
    for src, width, dst in _W_IN_SEGMENTS:
        o_ref[:, dst:dst + width] = w_ref[0, :, src:src + width].astype(o_ref.dtype)


def _relayout_w_in(w_in):
    _, D, n_in = w_in.shape
    tr = 256
    return pl.pallas_call(
        _relayout_kernel,
        grid=(D // tr,),
        in_specs=[pl.BlockSpec((1, tr, n_in), lambda i: (0, i, 0))],
        out_specs=pl.BlockSpec((tr, IN_COLS_P), lambda i: (i, 0)),
        out_shape=jax.ShapeDtypeStruct((D, IN_COLS_P), BF16),
        compiler_params=_cparams(("parallel",)),
        name="relayout_w_in",
    )(w_in)


def kernel(x, c, ada_w, ada_b, norm1_g, w_in, rwkv_mu, rwkv_w0, rwkv_w2, rwkv_a0, rwkv_a2, rwkv_g2, rwkv_kk, rwkv_ka, rwkv_rk, rwkv_ln_w, rwkv_ln_b, mlstm_conv_w, mlstm_conv_b, mlstm_b_i, mlstm_b_f, mlstm_norm_g, w_out, norm2_g, router_w, router_b, moe_w_gu, moe_b_gu, moe_w_dn, moe_b_dn, final_g):
    B, T, D = x.shape
    N = B * T
    x2 = x.reshape(N, D)
    l = 0

    mod = _adaln(c, ada_w[l], ada_b[l])
    sh_m, sc_m, gt_m, sh_f, sc_f, gt_f = [m.reshape(B, 1, D) for m in jnp.split(mod, 6, axis=-1)]

    proj = _in_proj(x2, norm1_g[l], sc_m, sh_m, _relayout_w_in(w_in), T)

    mu = rwkv_mu[l]
    mu_r, mu_wl, mu_k, mu_v, mu_al, mu_gl = (mu[0:1024], mu[1024:1088], mu[1088:2112], mu[2112:3136],
                                             mu[3136:3200], mu[3200:3360])
    row = lambda a: a.reshape(1, -1)
    zrows = lambda n: jnp.zeros((n, RWKV_DIM), F32)
    rw = {
        "mu_r": row(mu_r), "mu_k": row(mu_k), "mu_v": row(mu_v),
        "mu_gl": row(jnp.concatenate([mu_gl, jnp.zeros((96,), F32)])),
        "mu_lo": row(jnp.concatenate([mu_wl, mu_al])),
        "w0": row(rwkv_w0[l]), "a0": row(rwkv_a0[l]), "kk": row(rwkv_kk[l]), "ka": row(rwkv_ka[l]),
        "rk": row(rwkv_rk[l]), "ln_w": row(rwkv_ln_w[l]), "ln_b": row(rwkv_ln_b[l]),
        "w2": jnp.concatenate([rwkv_w2[l], zrows(A_LORA)], axis=0),
        "a2": jnp.concatenate([zrows(DECAY_LORA), rwkv_a2[l]], axis=0),
        "g2": jnp.concatenate([rwkv_g2[l], zrows(256 - G_LORA)], axis=0).astype(BF16),
    }
    y_rwkv = _rwkv(proj, B, T, rw)

    cw = mlstm_conv_w[l]
    cb = mlstm_conv_b[l]
    ml = {
        "cw_q": cw[:, :MLSTM_DIM], "cw_k": cw[:, MLSTM_DIM:],
        "cb_q": row(cb[:MLSTM_DIM]), "cb_k": row(cb[MLSTM_DIM:]),
        "gbias": row(jnp.concatenate([mlstm_b_i[l], mlstm_b_f[l], jnp.zeros((120,), F32)])),
        "norm_g": row(mlstm_norm_g[l]),
    }
    y_mlstm = _mlstm(proj, B, T, ml)

    rw_p = jnp.concatenate([router_w[l], jnp.zeros((D, 128 - N_EXPERTS), F32)], axis=1)
    rb_p = jnp.concatenate([router_b[l], jnp.zeros((128 - N_EXPERTS,), F32)]).reshape(1, 128)
    x1, h2, idx, gate_cols, rank, cnt = _out_route(
        y_rwkv, y_mlstm, x2, w_out[l].astype(BF16), gt_m, norm2_g[l], sc_f, sh_f, rw_p, rb_p, T)

    dest = _dest(idx, rank, cnt).reshape(-1)

    counts = cnt[:, 0]
    per = GROUP_ROWS // MOE_BLOCK
    nblk = (counts + MOE_BLOCK - 1) // MOE_BLOCK
    blk_start = jnp.cumsum(nblk) - nblk
    ngrp = (nblk + per - 1) // per
    gend = jnp.cumsum(ngrp)
    gstart = gend - ngrp
    n_groups = N_EXPERTS + (N * TOP_K) // GROUP_ROWS
    gid = jnp.arange(n_groups, dtype=jnp.int32)
    ge_raw = jnp.minimum(jnp.searchsorted(gend, gid, side="right"), N_EXPERTS - 1).astype(jnp.int32)
    valid = gid < gend[-1]
    last_e = jnp.minimum(jnp.searchsorted(gend, jnp.maximum(gend[-1] - 1, 0), side="right"),
                         N_EXPERTS - 1).astype(jnp.int32)
    ge = jnp.where(valid, ge_raw, last_e).astype(jnp.int32)
    within = gid - gstart[ge_raw]
    gs = jnp.where(valid, blk_start[ge_raw] + within * per, 0).astype(jnp.int32)
    gn = jnp.where(valid, jnp.minimum(nblk[ge_raw] - within * per, per), 0).astype(jnp.int32)
    gn = jnp.concatenate([gn, jnp.sum(nblk, keepdims=True).astype(jnp.int32)])

    n_rows = ((N * TOP_K) // MOE_BLOCK + N_EXPERTS) * MOE_BLOCK
    xs = _dispatch(dest, h2, n_rows)
    ys = _experts(ge, gs, gn, xs, moe_w_gu[l], moe_b_gu[l], moe_w_dn[l], moe_b_dn[l])
    out = _combine(dest, ys, x1, gate_cols, gt_f, final_g, T)
    return out.reshape(B, T, D)
```

```python
import functools

import jax
import jax.numpy as jnp
from jax import lax
from jax.experimental import pallas as pl
from jax.experimental.pallas import tpu as pltpu

F32 = jnp.float32
BF16 = jnp.bfloat16
HIGHEST = lax.Precision.HIGHEST

D_MODEL = 2048
CHUNK = 64
RMS_EPS = 1e-5
RWKV_DIM = 1024
RWKV_HEAD = 64
DECAY_LORA = 64
A_LORA = 64
G_LORA = 160
GN_EPS = 64e-5
MLSTM_DIM = 1024
MLSTM_HEADS = 4
MLSTM_HEAD = 256
CONV_W = 4
HEAD_NORM_EPS = 1e-6
N_EXPERTS = 32
TOP_K = 4
D_FF = 2048
SWIGLU_LIMIT = 7.0
SWIGLU_ALPHA = 1.702
MOE_BLOCK = 256

COL_R, COL_K, COL_V, COL_Q, COL_MK, COL_MV, COL_MO = (i * 1024 for i in range(7))
COL_GL = 7168
COL_LORA = 7424
COL_IF = 7552
IN_COLS_P = 7680
QUAD = 256

VMEM_LIMIT = 56 * 1024 * 1024


def _cparams(sem, vmem=VMEM_LIMIT):
    return pltpu.CompilerParams(dimension_semantics=sem, vmem_limit_bytes=vmem)


def _dot(a, b, **kw):
    return jnp.dot(a, b, preferred_element_type=F32, **kw)


def _dot_nt(a, b, **kw):
    return lax.dot_general(a, b, (((1,), (1,)), ((), ())), preferred_element_type=F32, **kw)


def _dot_tn(a, b, **kw):
    return lax.dot_general(a, b, (((0,), (0,)), ((), ())), preferred_element_type=F32, **kw)


def _split3(x):
    h = x.astype(BF16)
    r1 = x - h.astype(F32)
    m = r1.astype(BF16)
    l = (r1 - m.astype(F32)).astype(BF16)
    return h, m, l


def _iota(shape, dim):
    return lax.broadcasted_iota(jnp.int32, shape, dim)


def _adaln_kernel(c_ref, w_ref, b_ref, o_ref):
    c = c_ref[...]
    s = c * jax.nn.sigmoid(c)
    o_ref[...] = _dot(s, w_ref[...], precision=HIGHEST) + b_ref[...]


def _adaln(c, ada_w, ada_b):
    B, D = c.shape
    n_out = ada_w.shape[1]
    tn = 1024
    c8 = jnp.zeros((8, D), F32).at[:B].set(c)
    out = pl.pallas_call(
        _adaln_kernel,
        grid=(n_out // tn,),
        in_specs=[pl.BlockSpec((8, D), lambda j: (0, 0)),
                  pl.BlockSpec((D, tn), lambda j: (0, j)),
                  pl.BlockSpec((1, tn), lambda j: (0, j))],
        out_specs=pl.BlockSpec((8, tn), lambda j: (0, j)),
        out_shape=jax.ShapeDtypeStruct((8, n_out), F32),
        compiler_params=_cparams(("parallel",)),
        name="adaln",
    )(c8, ada_w, ada_b.reshape(1, n_out))
    return out[:B]


def _inproj_kernel(x_ref, g_ref, sc_ref, sh_ref, w_ref, o_ref, h_scr):
    @pl.when(pl.program_id(1) == 0)
    def _():
        x = x_ref[...]
        y = x * lax.rsqrt(jnp.mean(x * x, axis=-1, keepdims=True) + RMS_EPS)
        h = (y * g_ref[...]) * (1.0 + sc_ref[0]) + sh_ref[0]
        h_scr[...] = h.astype(BF16)

    o_ref[...] = _dot(h_scr[...], w_ref[...])


def _in_proj(x2, norm_g, sc, sh, w_p, seq):
    N, D = x2.shape
    tm, tn = 1024, 1536
    tiles_per_b = seq // tm
    ncols = w_p.shape[1]
    return pl.pallas_call(
        _inproj_kernel,
        grid=(N // tm, ncols // tn),
        in_specs=[pl.BlockSpec((tm, D), lambda i, j: (i, 0)),
                  pl.BlockSpec((1, D), lambda i, j: (0, 0)),
                  pl.BlockSpec((1, 1, D), lambda i, j: (i // tiles_per_b, 0, 0)),
                  pl.BlockSpec((1, 1, D), lambda i, j: (i // tiles_per_b, 0, 0)),
                  pl.BlockSpec((D, tn), lambda i, j: (0, j))],
        out_specs=pl.BlockSpec((tm, tn), lambda i, j: (i, j)),
        out_shape=jax.ShapeDtypeStruct((N, ncols), F32),
        scratch_shapes=[pltpu.VMEM((tm, D), BF16)],
        compiler_params=_cparams(("parallel", "arbitrary")),
        name="in_proj",
    )(x2, norm_g.reshape(1, D), sc, sh, w_p)


def _shift_rows(x, prev_row):
    rolled = pltpu.roll(x, 1, axis=0)
    row = _iota(x.shape, 0)
    return jnp.where(row == 0, prev_row, rolled)


def _rwkv_kernel(r_ref, k_ref, v_ref, gl_ref, lo_ref,
                 mu_r_ref, mu_k_ref, mu_v_ref, mu_gl_ref, mu_lo_ref,
                 w0_ref, a0_ref, kkp_ref, kap_ref, rk_ref, lnw_ref, lnb_ref,
                 w2_ref, a2_ref, g2_ref,
                 o_ref,
                 s_scr, prev_scr):
    L = CHUNK
    c = pl.program_id(1)

    @pl.when(c == 0)
    def _():
        s_scr[...] = jnp.zeros_like(s_scr)
        prev_scr[...] = jnp.zeros_like(prev_scr)

    def lerp(ref, mu_ref, lo, width):
        x = ref[...]
        prev = prev_scr[0:1, lo:lo + width]
        out = x + (_shift_rows(x, prev) - x) * mu_ref[...]
        prev_scr[0:1, lo:lo + width] = x[L - 1:L, :]
        return out

    r = lerp(r_ref, mu_r_ref, 0, RWKV_DIM)
    k = lerp(k_ref, mu_k_ref, 1024, RWKV_DIM)
    v = lerp(v_ref, mu_v_ref, 2048, RWKV_DIM)
    gl = lerp(gl_ref, mu_gl_ref, 3072, 256)
    lo = lerp(lo_ref, mu_lo_ref, 3328, 128)

    w = -jax.nn.softplus(-(w0_ref[...] + _dot(jnp.tanh(lo), w2_ref[...], precision=HIGHEST))) - 0.5
    lw = -jnp.exp(w)
    a = jax.nn.sigmoid(a0_ref[...] + _dot(lo, a2_ref[...], precision=HIGHEST))
    g = _dot(jax.nn.sigmoid(gl).astype(BF16), g2_ref[...])
    kkraw = k * kkp_ref[...]
    k2 = k * (1.0 + (a - 1.0) * kap_ref[...])

    ltri = jnp.where(_iota((L, L), 1) <= _iota((L, L), 0), 1.0, 0.0).astype(BF16)
    qr = _iota((QUAD, QUAD), 0)
    qc = _iota((QUAD, QUAD), 1)
    bd_mask = (qr // RWKV_HEAD) == (qc // RWKV_HEAD)
    bd = jnp.where(bd_mask, 1.0, 0.0).astype(BF16)
    strict_bd = bd_mask & ((qc % L) < (qr % L))
    incl_bd = bd_mask & ((qc % L) <= (qr % L))
    lane_head = _iota((1, QUAD), 1) // RWKV_HEAD
    n_heads = QUAD // RWKV_HEAD
    quads = range(RWKV_DIM // QUAD)

    def head_sum(x):
        h, m, l = _split3(x)
        return _dot(h, bd) + _dot(m, bd) + _dot(l, bd)

    def time_cumsum(x):
        h, m, l = _split3(x)
        return _dot(ltri, h) + _dot(ltri, m) + _dot(ltri, l)

    def stack_heads(x):
        return jnp.concatenate([jnp.where(lane_head == h, x, jnp.zeros_like(x)) for h in range(n_heads)], axis=0)

    def unstack_heads(x):
        out = x[0:L]
        for h in range(1, n_heads):
            out = out + x[h * L:(h + 1) * L]
        return out

    sls = [slice(q * QUAD, (q + 1) * QUAD) for q in quads]

    kk = []
    for sl in sls:
        kkr = kkraw[:, sl]
        kk.append(kkr / jnp.maximum(jnp.sqrt(head_sum(kkr * kkr)), 1e-12))
    cl = [time_cumsum(lw[:, sl]) for sl in sls]

    at, rt, bt, kt, p_last = [], [], [], [], []
    for q, sl in enumerate(sls):
        p_t = jnp.exp(cl[q])
        p_prev = jnp.exp(cl[q] - lw[:, sl])
        p_inv = jnp.exp(-cl[q])
        at.append((-kk[q] * p_prev).astype(BF16))
        rt.append((r[:, sl] * p_t).astype(BF16))
        bt.append((kk[q] * a[:, sl] * p_inv).astype(BF16))
        kt.append((k2[:, sl] * p_inv).astype(BF16))
        p_last.append(p_t[L - 1:L, :])

    s0 = [s_scr[q] for q in quads]
    n_pow, a_ak, a_rb, a_rk, rb4, rk4, vst, xs_st, y_state = [], [], [], [], [], [], [], [], []
    for q, sl in enumerate(sls):
        lst = jnp.concatenate([stack_heads(at[q]), stack_heads(rt[q])], axis=0)
        rb4.append(jnp.concatenate([bt[q]] * n_heads, axis=0))
        rk4.append(jnp.concatenate([kt[q]] * n_heads, axis=0))
        gb = _dot_nt(lst, rb4[q])
        gk = _dot_nt(lst, rk4[q])
        n_pow.append(jnp.where(strict_bd, gb[:QUAD], 0.0))
        a_rb.append(jnp.where(incl_bd, gb[QUAD:], 0.0).astype(BF16))
        a_ak.append(jnp.where(strict_bd, gk[:QUAD], 0.0).astype(BF16))
        a_rk.append(jnp.where(incl_bd, gk[QUAD:], 0.0).astype(BF16))
        vst.append(stack_heads(v[:, sl].astype(BF16)))
        xy = _dot_nt(jnp.concatenate([at[q], rt[q]], axis=0), s0[q].astype(BF16))
        xs_st.append(stack_heads(xy[:L]))
        y_state.append(xy[L:])

    u = [xs_st[q] + _dot(a_ak[q], vst[q]) for q in quads]
    for j in range(6):
        nb = [n_pow[q].astype(BF16) for q in quads]
        u = [u[q] + _dot(nb[q], u[q].astype(BF16)) for q in quads]
        if j < 5:
            n_pow = [_dot(nb[q], nb[q]) for q in quads]

    ub = [u[q].astype(BF16) for q in quads]
    y_st = [_dot(a_rb[q], ub[q]) + _dot(a_rk[q], vst[q]) for q in quads]
    for q in quads:
        z = s0[q] + _dot_tn(ub[q], rb4[q]) + _dot_tn(vst[q], rk4[q])
        s_scr[q] = jnp.where(bd_mask, z, 0.0) * p_last[q]

    inv_n = 1.0 / RWKV_HEAD
    y = [y_state[q] + unstack_heads(y_st[q]) for q in quads]
    mean = [head_sum(y[q]) * inv_n for q in quads]
    d = [y[q] - mean[q] for q in quads]
    var = [head_sum(d[q] * d[q]) * inv_n for q in quads]
    bonus = [head_sum(r[:, sl] * k2[:, sl] * rk_ref[:, sl]) * v[:, sl] for sl in sls]
    for q, sl in enumerate(sls):
        yn = d[q] * lax.rsqrt(var[q] + GN_EPS) * lnw_ref[:, sl] + lnb_ref[:, sl]
        o_ref[:, sl] = ((yn + bonus[q]) * g[:, sl]).astype(o_ref.dtype)


def _rwkv(proj, B, T, pr):
    NC = T // CHUNK
    L = CHUNK
    row = lambda b, c: b * NC + c

    def pspec(width, colblk):
        return pl.BlockSpec((L, width), lambda b, c: (row(b, c), colblk))

    def vec(width):
        return pl.BlockSpec((1, width), lambda b, c: (0, 0))

    def mat(rows):
        return pl.BlockSpec((rows, RWKV_DIM), lambda b, c: (0, 0))

    return pl.pallas_call(
        _rwkv_kernel,
        grid=(B, NC),
        in_specs=[pspec(1024, COL_R // 1024), pspec(1024, COL_K // 1024), pspec(1024, COL_V // 1024),
                  pspec(256, COL_GL // 256), pspec(128, COL_LORA // 128),
                  vec(1024), vec(1024), vec(1024), vec(256), vec(128),
                  vec(1024), vec(1024), vec(1024), vec(1024), vec(1024), vec(1024), vec(1024),
                  mat(128), mat(128), mat(256)],
        out_specs=pl.BlockSpec((L, RWKV_DIM), lambda b, c: (row(b, c), 0)),
        out_shape=jax.ShapeDtypeStruct((B * T, RWKV_DIM), BF16),
        scratch_shapes=[pltpu.VMEM((RWKV_DIM // QUAD, QUAD, QUAD), F32),
                        pltpu.VMEM((8, 3072 + 256 + 128), F32)],
        compiler_params=_cparams(("parallel", "arbitrary")),
        name="rwkv",
    )(proj, proj, proj, proj, proj,
      pr["mu_r"], pr["mu_k"], pr["mu_v"], pr["mu_gl"], pr["mu_lo"],
      pr["w0"], pr["a0"], pr["kk"], pr["ka"], pr["rk"], pr["ln_w"], pr["ln_b"],
      pr["w2"], pr["a2"], pr["g2"])


def _mlstm_kernel(q_ref, k_ref, v_ref, o_ref, gt_ref,
                  cwq_ref, cwk_ref, cbq_ref, cbk_ref, gbias_ref, ng_ref,
                  out_ref,
                  c_scr, n_scr, m_scr, prevq_scr, prevk_scr):
    L = CHUNK
    c = pl.program_id(1)

    @pl.when(c == 0)
    def _():
        c_scr[...] = jnp.zeros_like(c_scr)
        n_scr[...] = jnp.zeros_like(n_scr)
        m_scr[...] = jnp.zeros_like(m_scr)
        prevq_scr[...] = jnp.zeros_like(prevq_scr)
        prevk_scr[...] = jnp.zeros_like(prevk_scr)

    def conv_silu(ref, prev_scr, cw_ref, cb_ref):
        x = ref[...]
        prev8 = prev_scr[...]
        row8 = _iota(prev8.shape, 0)
        out = cb_ref[...] + x * cw_ref[CONV_W - 1:CONV_W, :]
        for d in range(1, CONV_W):
            rolled = pltpu.roll(x, d, axis=0)
            first8 = jnp.where(row8 < d, pltpu.roll(prev8, d, axis=0), rolled[:8])
            xs = jnp.concatenate([first8, rolled[8:]], axis=0)
            out = out + xs * cw_ref[CONV_W - 1 - d:CONV_W - d, :]
        prev_scr[...] = x[L - 8:L, :]
        return out * jax.nn.sigmoid(out)

    qf = conv_silu(q_ref, prevq_scr, cwq_ref, cbq_ref)
    kf = conv_silu(k_ref, prevk_scr, cwk_ref, cbk_ref) * (MLSTM_HEAD ** -0.5)
    vf = v_ref[...]

    z = gt_ref[...] + gbias_ref[...]
    lane = _iota(z.shape, 1)
    gc = jnp.where(lane < MLSTM_HEADS, z,
                   jnp.where(lane < 2 * MLSTM_HEADS, jax.nn.log_sigmoid(z), 0.0))
    row = _iota((L, L), 0)
    col = _iota((L, L), 1)
    causal = col <= row
    ltri = jnp.where(causal, 1.0, 0.0).astype(BF16)
    gh, gm, glo = _split3(gc)
    cum = _dot(ltri, gh) + _dot(ltri, gm) + _dot(ltri, glo)
    gc_t = gc.T
    cum_t = cum.T

    for h in range(MLSTM_HEADS):
        sl = slice(h * MLSTM_HEAD, (h + 1) * MLSTM_HEAD)
        qh, kh, vh = qf[:, sl], kf[:, sl], vf[:, sl]
        li_row = gc_t[h:h + 1, :]
        li_col = gc[:, h:h + 1]
        b_row = cum_t[MLSTM_HEADS + h:MLSTM_HEADS + h + 1, :]
        b_col = cum[:, MLSTM_HEADS + h:MLSTM_HEADS + h + 1]
        m_prev = m_scr[h:h + 1, 0:1]
        c_prev = c_scr[h]
        n_prev = n_scr[h:h + 1, :]

        a_inter = b_col + m_prev
        dm = jnp.where(causal, b_col - b_row + li_row, -jnp.inf)
        m_t = jnp.maximum(a_inter, jnp.max(dm, axis=-1, keepdims=True))
        w_inter = jnp.exp(a_inter - m_t)
        wmat = jnp.exp(dm - m_t)
        qb, kb, vb = qh.astype(BF16), kh.astype(BF16), vh.astype(BF16)
        s = _dot_nt(qb, kb) * wmat
        num = w_inter * _dot_nt(qb, c_prev.astype(BF16)) + _dot(s.astype(BF16), vb)
        den = w_inter * jnp.sum(qh * n_prev, axis=-1, keepdims=True) + jnp.sum(s, axis=-1, keepdims=True)
        hh = num / jnp.maximum(jnp.abs(den), jnp.exp(-m_t))

        m_new = m_t[L - 1:L, :]
        b_last = b_col[L - 1:L, :]
        g_state = jnp.exp(b_last + m_prev - m_new)
        w_s = jnp.exp(b_last - b_col + li_col - m_new)
        c_scr[h] = g_state * c_prev + _dot_tn((vh * w_s).astype(BF16), kb)
        n_scr[h:h + 1, :] = g_state * n_prev + jnp.sum(kh * w_s, axis=0, keepdims=True)
        m_scr[h:h + 1, :] = jnp.broadcast_to(m_new, (1, m_scr.shape[1]))

        hn = hh * lax.rsqrt(jnp.mean(hh * hh, axis=-1, keepdims=True) + HEAD_NORM_EPS)
        out_ref[:, sl] = ((hn * ng_ref[:, sl]) * jax.nn.sigmoid(o_ref[:, sl])).astype(out_ref.dtype)


def _mlstm(proj, B, T, pr):
    NC = T // CHUNK
    L = CHUNK
    row = lambda b, c: b * NC + c

    def pspec(width, colblk):
        return pl.BlockSpec((L, width), lambda b, c: (row(b, c), colblk))

    def cst(shape):
        return pl.BlockSpec(shape, lambda b, c: (0, 0))

    return pl.pallas_call(
        _mlstm_kernel,
        grid=(B, NC),
        in_specs=[pspec(1024, COL_Q // 1024), pspec(1024, COL_MK // 1024), pspec(1024, COL_MV // 1024),
                  pspec(1024, COL_MO // 1024), pspec(128, COL_IF // 128),
                  cst((CONV_W, 1024)), cst((CONV_W, 1024)), cst((1, 1024)), cst((1, 1024)),
                  cst((1, 128)), cst((1, 1024))],
        out_specs=pl.BlockSpec((L, MLSTM_DIM), lambda b, c: (row(b, c), 0)),
        out_shape=jax.ShapeDtypeStruct((B * T, MLSTM_DIM), BF16),
        scratch_shapes=[pltpu.VMEM((MLSTM_HEADS, MLSTM_HEAD, MLSTM_HEAD), F32),
                        pltpu.VMEM((8, MLSTM_HEAD), F32),
                        pltpu.VMEM((8, 128), F32),
                        pltpu.VMEM((8, 1024), F32),
                        pltpu.VMEM((8, 1024), F32)],
        compiler_params=_cparams(("parallel", "arbitrary")),
        name="mlstm",
    )(proj, proj, proj, proj, proj,
      pr["cw_q"], pr["cw_k"], pr["cb_q"], pr["cb_k"], pr["gbias"], pr["norm_g"])


def _out_route_kernel(yr_ref, ym_ref, x_ref, wo_ref, gt_ref, g2_ref, sc_ref, sh_ref, rw_ref, rb_ref,
                      x1_ref, h2_ref, idx_ref, gate_ref, rank_ref, cnt_ref,
                      carry_scr):
    i = pl.program_id(0)
    tm = x_ref.shape[0]

    @pl.when(i == 0)
    def _():
        carry_scr[...] = jnp.zeros_like(carry_scr)

    mix = _dot(yr_ref[...], wo_ref[0:RWKV_DIM, :]) + _dot(ym_ref[...], wo_ref[RWKV_DIM:, :])
    x1 = x_ref[...] + gt_ref[0] * mix
    x1_ref[...] = x1
    y = x1 * lax.rsqrt(jnp.mean(x1 * x1, axis=-1, keepdims=True) + RMS_EPS)
    h2 = (y * g2_ref[...]) * (1.0 + sc_ref[0]) + sh_ref[0]
    hbits = lax.bitcast_convert_type(h2.astype(BF16).astype(F32), jnp.uint32)
    half = h2.shape[1] // 2
    h2_ref[...] = (hbits[:, :half] >> 16) | (hbits[:, half:] & jnp.uint32(0xFFFF0000))

    logits = _dot(h2, rw_ref[...], precision=HIGHEST) + rb_ref[...]
    lt = logits.T[:N_EXPERTS, :]
    e_iota = _iota(lt.shape, 0)
    onehots, vals, idxs = [], [], []
    for _ in range(TOP_K):
        mx = jnp.max(lt, axis=0, keepdims=True)
        idx = jnp.min(jnp.where(lt == mx, e_iota, N_EXPERTS), axis=0, keepdims=True)
        sel = e_iota == idx
        onehots.append(sel)
        vals.append(mx)
        idxs.append(idx)
        lt = jnp.where(sel, -jnp.inf, lt)
    exps = [jnp.exp(vv - vals[0]) for vv in vals]
    denom = exps[0] + exps[1] + exps[2] + exps[3]
    gates = [e / denom for e in exps]

    member = jnp.zeros(onehots[0].shape, F32)
    for sel in onehots:
        member = member + jnp.where(sel, 1.0, 0.0)
    ur = _iota((tm, tm), 0)
    uc = _iota((tm, tm), 1)
    ustrict = jnp.where(ur < uc, 1.0, 0.0).astype(BF16)
    before = _dot(member.astype(BF16), ustrict) + carry_scr[:, 0:1]
    for j in range(TOP_K):
        rank = jnp.sum(jnp.where(onehots[j], before, 0.0), axis=0, keepdims=True)
        rank_ref[j:j + 1, :] = rank.astype(jnp.int32)
        idx_ref[j:j + 1, :] = idxs[j]
    new_carry = carry_scr[...] + jnp.sum(member, axis=1, keepdims=True)
    carry_scr[...] = new_carry
    cnt_ref[...] = new_carry.astype(jnp.int32)

    grows = jnp.concatenate(gates + [jnp.zeros((128 - TOP_K, tm), F32)], axis=0)
    gate_ref[...] = grows.T


def _out_route(yr, ym, x2, w_out_b, gt, g2, sc, sh, rw_p, rb_p, seq):
    N, D = x2.shape
    tm = 512
    tiles_per_b = seq // tm
    bvec = pl.BlockSpec((1, 1, D), lambda i: (i // tiles_per_b, 0, 0))
    return pl.pallas_call(
        _out_route_kernel,
        grid=(N // tm,),
        in_specs=[pl.BlockSpec((tm, RWKV_DIM), lambda i: (i, 0)),
                  pl.BlockSpec((tm, MLSTM_DIM), lambda i: (i, 0)),
                  pl.BlockSpec((tm, D), lambda i: (i, 0)),
                  pl.BlockSpec((D, D), lambda i: (0, 0)),
                  bvec,
                  pl.BlockSpec((1, D), lambda i: (0, 0)),
                  bvec, bvec,
                  pl.BlockSpec((D, 128), lambda i: (0, 0)),
                  pl.BlockSpec((1, 128), lambda i: (0, 0))],
        out_specs=[pl.BlockSpec((tm, D), lambda i: (i, 0)),
                   pl.BlockSpec((tm, D // 2), lambda i: (i, 0)),
                   pl.BlockSpec((TOP_K, tm), lambda i: (0, i)),
                   pl.BlockSpec((tm, 128), lambda i: (i, 0)),
                   pl.BlockSpec((TOP_K, tm), lambda i: (0, i)),
                   pl.BlockSpec((N_EXPERTS, 128), lambda i: (0, 0))],
        out_shape=[jax.ShapeDtypeStruct((N, D), F32),
                   jax.ShapeDtypeStruct((N, D // 2), jnp.uint32),
                   jax.ShapeDtypeStruct((TOP_K, N), jnp.int32),
                   jax.ShapeDtypeStruct((N, 128), F32),
                   jax.ShapeDtypeStruct((TOP_K, N), jnp.int32),
                   jax.ShapeDtypeStruct((N_EXPERTS, 128), jnp.int32)],
        scratch_shapes=[pltpu.VMEM((N_EXPERTS, 128), F32)],
        compiler_params=_cparams(("arbitrary",)),
        name="out_route",
    )(yr, ym, x2, w_out_b, gt, g2.reshape(1, D), sc, sh, rw_p, rb_p)


def _dest_kernel(idx_ref, rank_ref, cnt_ref, dest_ref):
    cnt = cnt_ref[:, 0:1].astype(F32)
    padded = jnp.floor((cnt + (MOE_BLOCK - 1)) * (1.0 / MOE_BLOCK)) * MOE_BLOCK
    er = _iota((N_EXPERTS, N_EXPERTS), 0)
    ec = _iota((N_EXPERTS, N_EXPERTS), 1)
    before = jnp.where(ec < er, 1.0, 0.0)
    pstart = _dot(before, jnp.broadcast_to(padded, (N_EXPERTS, 128)), precision=HIGHEST)[:, 0:1]
    e_iota = _iota((N_EXPERTS, idx_ref.shape[1]), 0)
    for j in range(TOP_K):
        sel = e_iota == idx_ref[j:j + 1, :]
        base = jnp.sum(jnp.where(sel, pstart, 0.0), axis=0, keepdims=True)
        dest_ref[j:j + 1, :] = base.astype(jnp.int32) + rank_ref[j:j + 1, :]


def _dest(idx, rank, cnt):
    N = idx.shape[1]
    tn = 2048
    return pl.pallas_call(
        _dest_kernel,
        grid=(N // tn,),
        in_specs=[pl.BlockSpec((TOP_K, tn), lambda i: (0, i)),
                  pl.BlockSpec((TOP_K, tn), lambda i: (0, i)),
                  pl.BlockSpec((N_EXPERTS, 128), lambda i: (0, 0))],
        out_specs=pl.BlockSpec((TOP_K, tn), lambda i: (0, i)),
        out_shape=jax.ShapeDtypeStruct((TOP_K, N), jnp.int32),
        compiler_params=_cparams(("parallel",)),
        name="dest",
    )(idx, rank, cnt)


def _dispatch_kernel(dest_ref, h_ref, xs_in_ref, xs_ref, sem):
    del xs_in_ref
    i = pl.program_id(0)
    tm = h_ref.shape[0]
    n_tok = pl.num_programs(0) * tm

    def copy(t, j):
        d = dest_ref[j * n_tok + i * tm + t]
        return pltpu.make_async_copy(h_ref.at[pl.ds(t, 1)], xs_ref.at[pl.ds(d, 1)], sem)

    def issue(t, carry):
        for j in range(TOP_K):
            copy(t, j).start()
        return carry

    lax.fori_loop(0, tm, issue, 0)

    def drain(t, carry):
        for j in range(TOP_K):
            copy(t, j).wait()
        return carry

    lax.fori_loop(0, tm, drain, 0)


def _dispatch(dest_flat, h2, n_rows):
    N, D = h2.shape
    tm = 512
    xs0 = jnp.zeros((n_rows, D), h2.dtype)
    return pl.pallas_call(
        _dispatch_kernel,
        grid_spec=pltpu.PrefetchScalarGridSpec(
            num_scalar_prefetch=1,
            grid=(N // tm,),
            in_specs=[pl.BlockSpec((tm, D), lambda i, d: (i, 0)),
                      pl.BlockSpec(memory_space=pl.ANY)],
            out_specs=pl.BlockSpec(memory_space=pl.ANY),
            scratch_shapes=[pltpu.SemaphoreType.DMA(())]),
        out_shape=jax.ShapeDtypeStruct((n_rows, D), h2.dtype),
        input_output_aliases={2: 0},
        compiler_params=_cparams(("arbitrary",)),
        name="dispatch",
    )(dest_flat, h2, xs0)


GROUP_ROWS = 2048
FF_TILE = 256
FFN_SPAN = 4


def _experts_kernel(ge_ref, gs_ref, gn_ref,
                    xs_ref, wg_ref, wu_ref, wd_ref, bg_ref, bu_ref, bd_ref,
                    ys_ref,
                    xbuf, xb16, acc, wgb, wub, wdb, sem):
    g = pl.program_id(0)
    j = pl.program_id(1)
    nblk = gn_ref[g]
    row0 = pl.multiple_of(gs_ref[g] * MOE_BLOCK, MOE_BLOCK)

    def in_copy(s):
        off = pl.multiple_of(s * MOE_BLOCK, MOE_BLOCK)
        return pltpu.make_async_copy(xs_ref.at[pl.ds(row0 + off, MOE_BLOCK)],
                                     xbuf.at[pl.ds(off, MOE_BLOCK)], sem.at[0])

    def out_copy(s):
        off = pl.multiple_of(s * MOE_BLOCK, MOE_BLOCK)
        return pltpu.make_async_copy(acc.at[pl.ds(off, MOE_BLOCK)],
                                     ys_ref.at[pl.ds(row0 + off, MOE_BLOCK)], sem.at[1])

    def for_blocks(fn):
        def body(s, carry):
            fn(s)
            return carry
        lax.fori_loop(0, nblk, body, 0)

    @pl.when(nblk > 0)
    def _():
        @pl.when(j == 0)
        def _():
            for_blocks(lambda s: in_copy(s).start())
            for_blocks(lambda s: in_copy(s).wait())

        wgb[...] = wg_ref[0].astype(BF16)
        wub[...] = wu_ref[0].astype(BF16)
        wdb[...] = wd_ref[0].astype(BF16)

        @pl.when(j == 0)
        def _():
            def init(s):
                rows = pl.ds(pl.multiple_of(s * MOE_BLOCK, MOE_BLOCK), MOE_BLOCK)
                w = xbuf[rows, :]
                half = w.shape[1]
                xb16[rows, 0:half] = lax.bitcast_convert_type(w << 16, F32).astype(BF16)
                xb16[rows, half:] = lax.bitcast_convert_type(w & jnp.uint32(0xFFFF0000), F32).astype(BF16)
                acc[rows, :] = jnp.broadcast_to(bd_ref[0], (MOE_BLOCK, acc.shape[1]))

            for_blocks(init)

        def ffn(row_start, m):
            rows = pl.ds(pl.multiple_of(row_start, MOE_BLOCK), m)
            xb = xb16[rows, :]
            gate = _dot(xb, wgb[...]) + bg_ref[0]
            up = _dot(xb, wub[...]) + bu_ref[0]
            gate = jnp.minimum(gate, SWIGLU_LIMIT)
            up = jnp.clip(up, -SWIGLU_LIMIT, SWIGLU_LIMIT)
            act = (up + 1.0) * (gate * jax.nn.sigmoid(SWIGLU_ALPHA * gate))
            acc[rows, :] = acc[rows, :] + _dot(act.astype(BF16), wdb[...])

        n_span = nblk // FFN_SPAN

        def span_body(s, carry):
            ffn(s * (FFN_SPAN * MOE_BLOCK), FFN_SPAN * MOE_BLOCK)
            return carry

        lax.fori_loop(0, n_span, span_body, 0)

        done = n_span * FFN_SPAN
        pair = (nblk - done) >= 2

        @pl.when(pair)
        def _():
            ffn(done * MOE_BLOCK, 2 * MOE_BLOCK)

        def rest_body(s, carry):
            ffn(s * MOE_BLOCK, MOE_BLOCK)
            return carry

        lax.fori_loop(done + jnp.where(pair, 2, 0), nblk, rest_body, 0)

        @pl.when(j == pl.num_programs(1) - 1)
        def _():
            for_blocks(lambda s: out_copy(s).start())
            for_blocks(lambda s: out_copy(s).wait())

    @pl.when((g == pl.num_programs(0) - 1) & (j == pl.num_programs(1) - 1))
    def _():
        acc[0:MOE_BLOCK, :] = jnp.zeros((MOE_BLOCK, acc.shape[1]), acc.dtype)

        def tail_copy(s):
            off = pl.multiple_of(s * MOE_BLOCK, MOE_BLOCK)
            return pltpu.make_async_copy(acc.at[pl.ds(0, MOE_BLOCK)], ys_ref.at[pl.ds(off, MOE_BLOCK)], sem.at[1])

        def over_tail(fn):
            def body(s, carry):
                fn(s)
                return carry
            lax.fori_loop(gn_ref[pl.num_programs(0)], ys_ref.shape[0] // MOE_BLOCK, body, 0)

        over_tail(lambda s: tail_copy(s).start())
        over_tail(lambda s: tail_copy(s).wait())


def _experts(ge, gs, gn, xs, w_gu, b_gu, w_dn, b_dn):
    R = xs.shape[0]
    D = D_MODEL
    E = w_gu.shape[0]
    G = ge.shape[0]
    J = D_FF // FF_TILE

    def jj(g, j, gn_ref):
        return jnp.where(gn_ref[g] > 0, j, J - 1)

    return pl.pallas_call(
        _experts_kernel,
        grid_spec=pltpu.PrefetchScalarGridSpec(
            num_scalar_prefetch=3,
            grid=(G, J),
            in_specs=[pl.BlockSpec(memory_space=pl.ANY),
                      pl.BlockSpec((1, D, FF_TILE), lambda g, j, ge, gs, gn: (ge[g], 0, jj(g, j, gn))),
                      pl.BlockSpec((1, D, FF_TILE), lambda g, j, ge, gs, gn: (ge[g], 0, J + jj(g, j, gn))),
                      pl.BlockSpec((1, FF_TILE, D), lambda g, j, ge, gs, gn: (ge[g], jj(g, j, gn), 0)),
                      pl.BlockSpec((1, 1, FF_TILE), lambda g, j, ge, gs, gn: (ge[g], 0, jj(g, j, gn))),
                      pl.BlockSpec((1, 1, FF_TILE), lambda g, j, ge, gs, gn: (ge[g], 0, J + jj(g, j, gn))),
                      pl.BlockSpec((1, 1, D), lambda g, j, ge, gs, gn: (ge[g], 0, 0))],
            out_specs=pl.BlockSpec(memory_space=pl.ANY),
            scratch_shapes=[pltpu.VMEM((GROUP_ROWS, D // 2), jnp.uint32),
                            pltpu.VMEM((GROUP_ROWS, D), BF16),
                            pltpu.VMEM((GROUP_ROWS, D), F32),
                            pltpu.VMEM((D, FF_TILE), BF16),
                            pltpu.VMEM((D, FF_TILE), BF16),
                            pltpu.VMEM((FF_TILE, D), BF16),
                            pltpu.SemaphoreType.DMA((2,))]),
        out_shape=jax.ShapeDtypeStruct((R, D), F32),
        compiler_params=_cparams(("arbitrary", "arbitrary")),
        name="experts",
    )(ge, gs, gn, xs, w_gu, w_gu, w_dn, b_gu.reshape(E, 1, 2 * D_FF), b_gu.reshape(E, 1, 2 * D_FF),
      b_dn.reshape(E, 1, D))


def _combine_kernel(dest_ref, ys_ref, x1_ref, gate_ref, gt_ref, fg_ref, o_ref, ybuf, sem):
    i = pl.program_id(0)
    tm = x1_ref.shape[0]
    n_tok = pl.num_programs(0) * tm

    def copy(t, j):
        d = dest_ref[j * n_tok + i * tm + t]
        return pltpu.make_async_copy(ys_ref.at[pl.ds(d, 1)], ybuf.at[j, pl.ds(t, 1)], sem)

    def issue(t, carry):
        for j in range(TOP_K):
            copy(t, j).start()
        return carry

    lax.fori_loop(0, tm, issue, 0)

    def drain(t, carry):
        for j in range(TOP_K):
            copy(t, j).wait()
        return carry

    lax.fori_loop(0, tm, drain, 0)

    gts = gate_ref[...]
    y = ybuf[0] * gts[:, 0:1]
    for j in range(1, TOP_K):
        y = y + ybuf[j] * gts[:, j:j + 1]
    x2 = x1_ref[...] + gt_ref[0] * y
    o_ref[...] = (x2 * lax.rsqrt(jnp.mean(x2 * x2, axis=-1, keepdims=True) + RMS_EPS)) * fg_ref[...]


def _combine(dest_flat, ys, x1, gate_cols, gt, final_g, seq):
    N, D = x1.shape
    tm = 512
    tiles_per_b = seq // tm
    return pl.pallas_call(
        _combine_kernel,
        grid_spec=pltpu.PrefetchScalarGridSpec(
            num_scalar_prefetch=1,
            grid=(N // tm,),
            in_specs=[pl.BlockSpec(memory_space=pl.ANY),
                      pl.BlockSpec((tm, D), lambda i, d: (i, 0)),
                      pl.BlockSpec((tm, 128), lambda i, d: (i, 0)),
                      pl.BlockSpec((1, 1, D), lambda i, d: (i // tiles_per_b, 0, 0)),
                      pl.BlockSpec((1, D), lambda i, d: (0, 0))],
            out_specs=pl.BlockSpec((tm, D), lambda i, d: (i, 0)),
            scratch_shapes=[pltpu.VMEM((TOP_K, tm, D), F32),
                            pltpu.SemaphoreType.DMA(())]),
        out_shape=jax.ShapeDtypeStruct((N, D), F32),
        compiler_params=_cparams(("arbitrary",)),
        name="combine",
    )(dest_flat, ys, x1, gate_cols, gt, final_g.reshape(1, D))


_W_IN_SEGMENTS = (
    (0, 1024, COL_R), (1088, 1024, COL_K), (2112, 1024, COL_V),
    (3360, 1024, COL_Q), (4384, 1024, COL_MK), (5408, 1024, COL_MV), (6432, 1024, COL_MO),
    (3200, 160, COL_GL), (1024, 64, COL_LORA), (3136, 64, COL_LORA + 64),
    (7456, 8, COL_IF),
)


def _relayout_kernel(w_ref, o_ref):
    o_ref[...] = jnp.zeros(o_ref.shape, o_ref.dtype)
    for src, width, dst in _W_IN_SEGMENTS:
        o_ref[:, dst:dst + width] = w_ref[0, :, src:src + width].astype(o_ref.dtype)


def _relayout_w_in(w_in):
    _, D, n_in = w_in.shape
    tr = 256
    return pl.pallas_call(
        _relayout_kernel,
        grid=(D // tr,),
        in_specs=[pl.BlockSpec((1, tr, n_in), lambda i: (0, i, 0))],
        out_specs=pl.BlockSpec((tr, IN_COLS_P), lambda i: (i, 0)),
        out_shape=jax.ShapeDtypeStruct((D, IN_COLS_P), BF16),
        compiler_params=_cparams(("parallel",)),
        name="relayout_w_in",
    )(w_in)


def kernel(x, c, ada_w, ada_b, norm1_g, w_in, rwkv_mu, rwkv_w0, rwkv_w2, rwkv_a0, rwkv_a2, rwkv_g2, rwkv_kk, rwkv_ka, rwkv_rk, rwkv_ln_w, rwkv_ln_b, mlstm_conv_w, mlstm_conv_b, mlstm_b_i, mlstm_b_f, mlstm_norm_g, w_out, norm2_g, router_w, router_b, moe_w_gu, moe_b_gu, moe_w_dn, moe_b_dn, final_g):
    B, T, D = x.shape
    N = B * T
    x2 = x.reshape(N, D)
    l = 0

    mod = _adaln(c, ada_w[l], ada_b[l])
    sh_m, sc_m, gt_m, sh_f, sc_f, gt_f = [m.reshape(B, 1, D) for m in jnp.split(mod, 6, axis=-1)]

    proj = _in_proj(x2, norm1_g[l], sc_m, sh_m, _relayout_w_in(w_in), T)

    mu = rwkv_mu[l]
    mu_r, mu_wl, mu_k, mu_v, mu_al, mu_gl = (mu[0:1024], mu[1024:1088], mu[1088:2112], mu[2112:3136],
                                             mu[3136:3200], mu[3200:3360])
    row = lambda a: a.reshape(1, -1)
    zrows = lambda n: jnp.zeros((n, RWKV_DIM), F32)
    rw = {
        "mu_r": row(mu_r), "mu_k": row(mu_k), "mu_v": row(mu_v),
        "mu_gl": row(jnp.concatenate([mu_gl, jnp.zeros((96,), F32)])),
        "mu_lo": row(jnp.concatenate([mu_wl, mu_al])),
        "w0": row(rwkv_w0[l]), "a0": row(rwkv_a0[l]), "kk": row(rwkv_kk[l]), "ka": row(rwkv_ka[l]),
        "rk": row(rwkv_rk[l]), "ln_w": row(rwkv_ln_w[l]), "ln_b": row(rwkv_ln_b[l]),
        "w2": jnp.concatenate([rwkv_w2[l], zrows(A_LORA)], axis=0),
        "a2": jnp.concatenate([zrows(DECAY_LORA), rwkv_a2[l]], axis=0),
        "g2": jnp.concatenate([rwkv_g2[l], zrows(256 - G_LORA)], axis=0).astype(BF16),
    }
    y_rwkv = _rwkv(proj, B, T, rw)

    cw = mlstm_conv_w[l]
    cb = mlstm_conv_b[l]
    ml = {
        "cw_q": cw[:, :MLSTM_DIM], "cw_k": cw[:, MLSTM_DIM:],
        "cb_q": row(cb[:MLSTM_DIM]), "cb_k": row(cb[MLSTM_DIM:]),
        "gbias": row(jnp.concatenate([mlstm_b_i[l], mlstm_b_f[l], jnp.zeros((120,), F32)])),
        "norm_g": row(mlstm_norm_g[l]),
    }
    y_mlstm = _mlstm(proj, B, T, ml)

    rw_p = jnp.concatenate([router_w[l], jnp.zeros((D, 128 - N_EXPERTS), F32)], axis=1)
    rb_p = jnp.concatenate([router_b[l], jnp.zeros((128 - N_EXPERTS,), F32)]).reshape(1, 128)
    x1, h2, idx, gate_cols, rank, cnt = _out_route(
        y_rwkv, y_mlstm, x2, w_out[l].astype(BF16), gt_m, norm2_g[l], sc_f, sh_f, rw_p, rb_p, T)

    dest = _dest(idx, rank, cnt).reshape(-1)

    counts = cnt[:, 0]
    per = GROUP_ROWS // MOE_BLOCK
    nblk = (counts + MOE_BLOCK - 1) // MOE_BLOCK
    blk_start = jnp.cumsum(nblk) - nblk
    ngrp = (nblk + per - 1) // per
    gend = jnp.cumsum(ngrp)
    gstart = gend - ngrp
    n_groups = N_EXPERTS + (N * TOP_K) // GROUP_ROWS
    gid = jnp.arange(n_groups, dtype=jnp.int32)
    ge_raw = jnp.minimum(jnp.searchsorted(gend, gid, side="right"), N_EXPERTS - 1).astype(jnp.int32)
    valid = gid < gend[-1]
    last_e = jnp.minimum(jnp.searchsorted(gend, jnp.maximum(gend[-1] - 1, 0), side="right"),
                         N_EXPERTS - 1).astype(jnp.int32)
    ge = jnp.where(valid, ge_raw, last_e).astype(jnp.int32)
    within = gid - gstart[ge_raw]
    gs = jnp.where(valid, blk_start[ge_raw] + within * per, 0).astype(jnp.int32)
    gn = jnp.where(valid, jnp.minimum(nblk[ge_raw] - within * per, per), 0).astype(jnp.int32)
    gn = jnp.concatenate([gn, jnp.sum(nblk, keepdims=True).astype(jnp.int32)])

    n_rows = ((N * TOP_K) // MOE_BLOCK + N_EXPERTS) * MOE_BLOCK
    xs = _dispatch(dest, h2, n_rows)
    ys = _experts(ge, gs, gn, xs, moe_w_gu[l], moe_b_gu[l], moe_w_dn[l], moe_b_dn[l])
    out = _combine(dest, ys, x1, gate_cols, gt_f, final_g, T)
    return out.reshape(B, T, D)
```
